```python
import math
import jax, jax.numpy as jnp
from jax import lax
import numpy as np

D_MODEL = 1024
BATCH = 2
SEQ = 8192
DEPTH = 2
DEC_BATCH = 32
DEC_SEQ = 1
PAST_LEN = 16384
PAGE_SIZE = 128

MIX_WIDTH = D_MODEL
HEAD_DIM = 64
ATTN_HEADS = (MIX_WIDTH // 2) // HEAD_DIM
ATTN_CH = ATTN_HEADS * HEAD_DIM
POOL_WINDOWS = (2, 4, 8, 16)
POOL_GROUPS = len(POOL_WINDOWS)
POOL_CH = MIX_WIDTH // 4
POOL_GW = POOL_CH // POOL_GROUPS
POOL_HIST = max(POOL_WINDOWS) - 1
CONV_CH = MIX_WIDTH - ATTN_CH - POOL_CH
CONV_TAPS = 31
CONV_HIST = CONV_TAPS - 1
IN_COLS = 3 * ATTN_CH + POOL_CH + 2 * CONV_CH
SPLITS = (ATTN_CH, 2 * ATTN_CH, 3 * ATTN_CH, 3 * ATTN_CH + POOL_CH, 3 * ATTN_CH + POOL_CH + CONV_CH)
D_FF = 4 * D_MODEL
Q_BLOCK = 128
SB_BIAS_INIT = -6.0
N_PAGES = PAST_LEN // PAGE_SIZE
N_USED_PAGES = DEC_BATCH * N_PAGES
N_PHYS_PAGES = N_USED_PAGES + max(1, N_USED_PAGES // 4)
EPS = 1e-6

kernel_name = "hymba_pool_conformer_stickbreak_decode_step"


def rmsnorm(x, g):
    xf = x.astype(jnp.float32)
    y = xf * lax.rsqrt(jnp.mean(xf * xf, axis=-1, keepdims=True) + EPS)
    return (y * g).astype(x.dtype)


def mix_inputs(x, norm_g, w_in, q_g, k_g):
    B, T, _ = x.shape
    z = rmsnorm(x, norm_g) @ w_in
    q, k, v, u, a, gate = jnp.split(z, SPLITS, axis=-1)
    q = rmsnorm(q.reshape(B, T, ATTN_HEADS, HEAD_DIM), q_g)
    k = rmsnorm(k.reshape(B, T, ATTN_HEADS, HEAD_DIM), k_g)
    v = v.reshape(B, T, ATTN_HEADS, HEAD_DIM)
    h = a * jax.nn.sigmoid(gate)
    return q, k, v, u, h


def stick_breaking(q, k, v, bias, q_pos, k_pos):
    z = jnp.einsum("bqhd,bkhd->bhqk", q, k).astype(jnp.float32) * (HEAD_DIM ** -0.5)
    z = z + bias.astype(jnp.float32)[None, :, None, None]
    visible = k_pos[None, :] < q_pos[:, None]
    log_keep = jnp.where(visible, jax.nn.log_sigmoid(-z), 0.0)
    log_after = lax.cumsum(log_keep, axis=3, reverse=True) - log_keep
    w = jnp.where(visible, jnp.exp(jax.nn.log_sigmoid(z) + log_after), 0.0)
    return jnp.einsum("bhqk,bkhd->bqhd", w.astype(v.dtype), v)


def stick_breaking_prompt(q, k, v, bias):
    B, T, H, Dh = q.shape
    nb = T // Q_BLOCK
    qb = q.reshape(B, nb, Q_BLOCK, H, Dh).transpose(1, 0, 2, 3, 4)
    k_pos = jnp.arange(T)

    def block(args):
        q_blk, i = args
        q_pos = i * Q_BLOCK + jnp.arange(Q_BLOCK)
        return stick_breaking(q_blk, k, v, bias, q_pos, k_pos)

    out = lax.map(block, (qb, jnp.arange(nb)))
    return out.transpose(1, 0, 2, 3, 4).reshape(B, T, H, Dh)


def pool_mix(u_ext, start_pos, pool_w, pool_scale):
    B, L, C = u_ext.shape
    T = L - POOL_HIST
    uf = u_ext.astype(jnp.float32)
    cz = jnp.concatenate([jnp.zeros((B, 1, C), jnp.float32), jnp.cumsum(uf, axis=1)], axis=1)
    pos = start_pos + jnp.arange(T)
    u_new = uf[:, POOL_HIST:]
    diffs = []
    for g, win in enumerate(POOL_WINDOWS):
        sl = slice(g * POOL_GW, (g + 1) * POOL_GW)
        s = cz[:, POOL_HIST + 1:POOL_HIST + 1 + T, sl] - cz[:, POOL_HIST + 1 - win:POOL_HIST + 1 - win + T, sl]
        cnt = jnp.minimum(win, pos + 1).astype(jnp.float32)[None, :, None]
        diffs.append(s / cnt - u_new[:, :, sl])
    d = jnp.stack(diffs, axis=2).astype(u_ext.dtype)
    y = jnp.einsum("btgc,gce->btge", d, pool_w).reshape(B, T, C)
    return y * pool_scale


def conv_mix(h_ext, dw_w, dw_b, ln_g, ln_b, pw_w, pw_b):
    C = h_ext.shape[-1]
    y = lax.conv_general_dilated(h_ext, dw_w[:, None, :], (1,), "VALID",
                                 dimension_numbers=("NWC", "WIO", "NWC"),
                                 feature_group_count=C) + dw_b
    yf = y.astype(jnp.float32)
    mu = jnp.mean(yf, axis=-1, keepdims=True)
    var = jnp.mean(jnp.square(yf - mu), axis=-1, keepdims=True)
    yn = (yf - mu) * lax.rsqrt(var + EPS) * ln_g + ln_b
    yn = jax.nn.silu(yn).astype(h_ext.dtype)
    return yn @ pw_w + pw_b


def merge_and_ffn(x, attn, pool, conv, w_out, norm_ffn_g, w_up, w_down):
    B, T, _ = x.shape
    m = jnp.concatenate([attn.reshape(B, T, ATTN_CH), pool, conv], axis=-1) @ w_out
    x = x + m
    h = rmsnorm(x, norm_ffn_g) @ w_up
    return x + jnp.square(jax.nn.relu(h)) @ w_down


def setup_inputs(seed: int = 0) -> dict:
    key = jax.random.key(seed)
    ks = jax.random.split(key, 26)
    f32 = jnp.float32
    nrm = lambda k, shape, s: jax.random.normal(k, shape, f32) * s
    page_table = jax.random.permutation(ks[6], N_PHYS_PAGES)[:N_USED_PAGES].reshape(DEC_BATCH, N_PAGES).astype(jnp.int32)
    return {
        "x_prompt": nrm(ks[0], (BATCH, SEQ, D_MODEL), 1.0),
        "x_sample": nrm(ks[1], (DEC_BATCH, DEC_SEQ, D_MODEL), 1.0),
        "cache_k": nrm(ks[2], (DEPTH, N_PHYS_PAGES, PAGE_SIZE, ATTN_HEADS, HEAD_DIM), 1.0),
        "cache_v": nrm(ks[3], (DEPTH, N_PHYS_PAGES, PAGE_SIZE, ATTN_HEADS, HEAD_DIM), 1.0),
        "state_pool": nrm(ks[4], (DEPTH, DEC_BATCH, POOL_HIST, POOL_CH), 1.0),
        "state_conv": nrm(ks[5], (DEPTH, DEC_BATCH, CONV_HIST, CONV_CH), 0.5),
        "page_table": page_table,
        "norm_mix_g": 1.0 + nrm(ks[7], (DEPTH, D_MODEL), 0.02),
        "w_in": nrm(ks[8], (DEPTH, D_MODEL, IN_COLS), D_MODEL ** -0.5),
        "q_norm_g": 1.0 + nrm(ks[9], (DEPTH, HEAD_DIM), 0.02),
        "k_norm_g": 1.0 + nrm(ks[10], (DEPTH, HEAD_DIM), 0.02),
        "sb_bias": SB_BIAS_INIT + nrm(ks[23], (DEPTH, ATTN_HEADS), 0.1),
        "pool_w": nrm(ks[11], (DEPTH, POOL_GROUPS, POOL_GW, POOL_GW), POOL_GW ** -0.5),
        "pool_scale": 1.0 + nrm(ks[12], (DEPTH, POOL_CH), 0.1),
        "conv_dw_w": nrm(ks[13], (DEPTH, CONV_TAPS, CONV_CH), CONV_TAPS ** -0.5),
        "conv_dw_b": nrm(ks[14], (DEPTH, CONV_CH), 0.01),
        "conv_ln_g": 1.0 + nrm(ks[15], (DEPTH, CONV_CH), 0.02),
        "conv_ln_b": nrm(ks[16], (DEPTH, CONV_CH), 0.01),
        "conv_pw_w": nrm(ks[17], (DEPTH, CONV_CH, CONV_CH), CONV_CH ** -0.5),
        "conv_pw_b": nrm(ks[18], (DEPTH, CONV_CH), 0.01),
        "w_out": nrm(ks[19], (DEPTH, MIX_WIDTH, D_MODEL), MIX_WIDTH ** -0.5),
        "norm_ffn_g": 1.0 + nrm(ks[20], (DEPTH, D_MODEL), 0.02),
        "w_ffn_up": nrm(ks[21], (DEPTH, D_MODEL, D_FF), D_MODEL ** -0.5),
        "w_ffn_down": nrm(ks[22], (DEPTH, D_FF, D_MODEL), D_FF ** -0.5),
    }


def reference(x_prompt, x_sample, cache_k, cache_v, state_pool, state_conv, page_table,
              norm_mix_g, w_in, q_norm_g, k_norm_g, sb_bias, pool_w, pool_scale,
              conv_dw_w, conv_dw_b, conv_ln_g, conv_ln_b, conv_pw_w, conv_pw_b,
              w_out, norm_ffn_g, w_ffn_up, w_ffn_down):
    B, T, _ = x_prompt.shape
    DB, TS, _ = x_sample.shape
    xp, xs = x_prompt, x_sample
    kp_l, vp_l, ks_l, vs_l, pp_l, ps_l, cp_l, cs_l = [], [], [], [], [], [], [], []
    for l in range(DEPTH):
        qp, kp, vp, up, hp = mix_inputs(xp, norm_mix_g[l], w_in[l], q_norm_g[l], k_norm_g[l])
        qs, kn, vn, us, hs = mix_inputs(xs, norm_mix_g[l], w_in[l], q_norm_g[l], k_norm_g[l])

        att_p = stick_breaking_prompt(qp, kp, vp, sb_bias[l])
        k_past = cache_k[l][page_table].reshape(DB, -1, ATTN_HEADS, HEAD_DIM)
        v_past = cache_v[l][page_table].reshape(DB, -1, ATTN_HEADS, HEAD_DIM)
        past = k_past.shape[1]
        k_all = jnp.concatenate([k_past, kn], axis=1)
        v_all = jnp.concatenate([v_past, vn], axis=1)
        att_s = stick_breaking(qs, k_all, v_all, sb_bias[l], past + jnp.arange(TS), jnp.arange(past + TS))

        up_ext = jnp.concatenate([jnp.zeros((B, POOL_HIST, POOL_CH), up.dtype), up], axis=1)
        us_ext = jnp.concatenate([state_pool[l].astype(us.dtype), us], axis=1)
        pool_p = pool_mix(up_ext, 0, pool_w[l], pool_scale[l])
        pool_s = pool_mix(us_ext, PAST_LEN, pool_w[l], pool_scale[l])

        hp_ext = jnp.concatenate([jnp.zeros((B, CONV_HIST, CONV_CH), hp.dtype), hp], axis=1)
        hs_ext = jnp.concatenate([state_conv[l].astype(hs.dtype), hs], axis=1)
        conv_p = conv_mix(hp_ext, conv_dw_w[l], conv_dw_b[l], conv_ln_g[l], conv_ln_b[l], conv_pw_w[l], conv_pw_b[l])
        conv_s = conv_mix(hs_ext, conv_dw_w[l], conv_dw_b[l], conv_ln_g[l], conv_ln_b[l], conv_pw_w[l], conv_pw_b[l])

        xp = merge_and_ffn(xp, att_p, pool_p, conv_p, w_out[l], norm_ffn_g[l], w_ffn_up[l], w_ffn_down[l])
        xs = merge_and_ffn(xs, att_s, pool_s, conv_s, w_out[l], norm_ffn_g[l], w_ffn_up[l], w_ffn_down[l])

        kp_l.append(kp); vp_l.append(vp); ks_l.append(kn); vs_l.append(vn)
        pp_l.append(up_ext[:, -POOL_HIST:]); ps_l.append(us_ext[:, -POOL_HIST:])
        cp_l.append(hp_ext[:, -CONV_HIST:]); cs_l.append(hs_ext[:, -CONV_HIST:])

    return (xp, xs,
            jnp.stack(kp_l), jnp.stack(vp_l), jnp.stack(ks_l), jnp.stack(vs_l),
            jnp.stack(pp_l), jnp.stack(ps_l), jnp.stack(cp_l), jnp.stack(cs_l))
```

```python
import functools

import jax
import jax.numpy as jnp
from jax import lax
from jax.experimental import pallas as pl
from jax.experimental.pallas import tpu as pltpu

F32 = jnp.float32
BF16 = jnp.bfloat16

EPS = 1e-6
HEAD_DIM = 64
POOL_WINDOWS = (2, 4, 8, 16)
POOL_HIST = max(POOL_WINDOWS) - 1
CONV_TAPS = 31
CONV_HIST = CONV_TAPS - 1
PAGE_SIZE = 128

V7X_VMEM_LIMIT_BYTES = 56 * 1024 * 1024
LANES = 128


def _params(*sem):
    return pltpu.CompilerParams(dimension_semantics=sem, vmem_limit_bytes=V7X_VMEM_LIMIT_BYTES)


def _const_spec(shape):
    nd = len(shape)
    return pl.BlockSpec(shape, lambda *_: (0,) * nd)


def _softplus(z):
    return jnp.maximum(z, 0.0) + jnp.log(1.0 + jnp.exp(-jnp.abs(z)))


def _inproj_kernel(x_ref, g_ref, w_ref, qg_ref, kg_ref, hm_ref,
                   q_ref, kf_ref, kb_ref, vf_ref, vb_ref, u_ref, h_ref, *, attn_ch, pool_ch, conv_ch):
    x = x_ref[...]
    ms = jnp.mean(x * x, axis=-1, keepdims=True)
    xn = (x * lax.rsqrt(ms + EPS) * g_ref[...]).astype(BF16)
    z = jnp.dot(xn, w_ref[...], preferred_element_type=F32)
    c0, c1, c2 = attn_ch, 2 * attn_ch, 3 * attn_ch
    c3 = c2 + pool_ch
    c4 = c3 + conv_ch
    q, k, v = z[:, :c0], z[:, c0:c1], z[:, c1:c2]
    u, a, gate = z[:, c2:c3], z[:, c3:c4], z[:, c4:]

    def head_norm(t, gain):
        msq = jnp.dot((t * t).astype(BF16), hm_ref[...], preferred_element_type=F32)
        return t * lax.rsqrt(msq + EPS) * gain

    qn = head_norm(q, qg_ref[...])
    kn = head_norm(k, kg_ref[...])
    q_ref[...] = (qn * (HEAD_DIM ** -0.5)).astype(BF16)
    kf_ref[...] = kn
    kb_ref[...] = kn.astype(BF16)
    vf_ref[...] = v
    vb_ref[...] = v.astype(BF16)
    u_ref[...] = u
    h_ref[...] = a * jax.nn.sigmoid(gate)


def _inproj(x, g, w_bf, qg_t, kg_t, head_mean, *, tm, attn_ch, pool_ch, conv_ch):
    m, d = x.shape
    ncol = w_bf.shape[1]
    row = lambda c: pl.BlockSpec((tm, c), lambda i: (i, 0))
    outs = [
        jax.ShapeDtypeStruct((m, attn_ch), BF16),
        jax.ShapeDtypeStruct((m, attn_ch), F32),
        jax.ShapeDtypeStruct((m, attn_ch), BF16),
        jax.ShapeDtypeStruct((m, attn_ch), F32),
        jax.ShapeDtypeStruct((m, attn_ch), BF16),
        jax.ShapeDtypeStruct((m, pool_ch), F32),
        jax.ShapeDtypeStruct((m, conv_ch), F32),
    ]
    return pl.pallas_call(
        functools.partial(_inproj_kernel, attn_ch=attn_ch, pool_ch=pool_ch, conv_ch=conv_ch),
        grid=(m // tm,),
        in_specs=[row(d), _const_spec((1, d)), _const_spec((d, ncol)),
                  _const_spec((1, attn_ch)), _const_spec((1, attn_ch)), _const_spec((attn_ch, attn_ch))],
        out_specs=[row(attn_ch)] * 5 + [row(pool_ch), row(conv_ch)],
        out_shape=outs,
        compiler_params=_params("parallel"),
        name="inproj",
    )(x, g, w_bf, qg_t, kg_t, head_mean)


def _attn_prompt_kernel(bias_ref, qt_ref, k_ref, vt_ref, tri_ref, o_ref):
    h = pl.program_id(1)
    i = pl.program_id(2)
    bias = bias_ref[h]
    qt = qt_ref[...]
    tri = tri_ref[...]
    tk, tq = tri.shape[0], qt.shape[1]

    def scores(j):
        return jnp.dot(k_ref[j], qt, preferred_element_type=F32) + bias

    z = scores(i)
    vis = lax.broadcasted_iota(jnp.int32, (tk, tq), 0) < lax.broadcasted_iota(jnp.int32, (tk, tq), 1)
    sp = jnp.where(vis, _softplus(z), 0.0)
    s = jnp.dot(tri, sp.astype(BF16), preferred_element_type=F32)
    w = jnp.where(vis, jnp.exp(z - s), 0.0)
    acc = jnp.dot(vt_ref[i], w.astype(BF16), preferred_element_type=F32)
    r = s[0:1, :]

    def body(jj, carry):
        acc, r = carry
        j = i - 1 - jj
        z = scores(j)
        s = jnp.dot(tri, _softplus(z).astype(BF16), preferred_element_type=F32)
        w = jnp.exp(z - s)
        pv = jnp.dot(vt_ref[j], w.astype(BF16), preferred_element_type=F32)
        return acc + pv * jnp.exp(-r), r + s[0:1, :]

    acc, _ = lax.fori_loop(0, i, body, (acc, r))
    o_ref[...] = acc


def _attn_prompt(qt, k_hm, vt_hm, tri, bias, *, tq):
    b, h, dh, t = qt.shape
    nk, tk = k_hm.shape[2], k_hm.shape[3]
    return pl.pallas_call(
        _attn_prompt_kernel,
        grid=(b, h, t // tq),
        in_specs=[
            pl.BlockSpec(memory_space=pltpu.SMEM),
            pl.BlockSpec((None, None, dh, tq), lambda bi, hi, i: (bi, hi, 0, i)),
            pl.BlockSpec((None, None, nk, tk, dh), lambda bi, hi, i: (bi, hi, 0, 0, 0)),
            pl.BlockSpec((None, None, nk, dh, tk), lambda bi, hi, i: (bi, hi, 0, 0, 0)),
            _const_spec((tk, tk)),
        ],
        out_specs=pl.BlockSpec((None, None, dh, tq), lambda bi, hi, i: (bi, hi, 0, i)),
        out_shape=jax.ShapeDtypeStruct((b, h, dh, t), F32),
        compiler_params=_params("parallel", "parallel", "arbitrary"),
        name="attn_prompt",
    )(bias, qt, k_hm, vt_hm, tri)


def _attn_decode_kernel(pt_ref, qbd_ref, bias_ref, tri_ref, ind_ref, *refs, pages):
    del pt_ref
    k_refs, v_refs = refs[:pages], refs[pages:2 * pages]
    o_ref, acc_ref, r_ref = refs[2 * pages:]
    j = pl.program_id(1)

    @pl.when(j == 0)
    def _():
        acc_ref[...] = jnp.zeros_like(acc_ref)
        r_ref[...] = jnp.zeros_like(r_ref)

    qbd = qbd_ref[...]
    tri = tri_ref[...]
    ind = ind_ref[...]
    bias = bias_ref[...]
    acc = acc_ref[...]
    r = r_ref[...]
    for p in reversed(range(pages)):
        z = jnp.dot(k_refs[p][...].astype(BF16), qbd, preferred_element_type=F32) + bias
        s = jnp.dot(tri, _softplus(z).astype(BF16), preferred_element_type=F32)
        w = jnp.exp(z - s - r)
        wx = jnp.dot(w.astype(BF16), ind, preferred_element_type=F32)
        c = wx * v_refs[p][...]
        acc = acc + c.reshape(c.shape[0] // 8, 8, c.shape[1]).sum(axis=0)
        r = r + s[0:1, :]
    acc_ref[...] = acc
    r_ref[...] = r

    @pl.when(j == pl.num_programs(1) - 1)
    def _():
        o_ref[...] = jnp.sum(acc, axis=0, keepdims=True)


def _attn_decode(page_table, qbd, bias_row, tri, ind, cache_k, cache_v, *, layer, pages):
    db, n_pages = page_table.shape
    ch = cache_k.shape[-1]
    page = cache_k.shape[2]
    groups = n_pages // pages

    def page_spec(p):
        return pl.BlockSpec(
            (None, None, page, ch),
            lambda b, j, pt: (layer, pt[b, (groups - 1 - j) * pages + p], 0, 0))

    grid_spec = pltpu.PrefetchScalarGridSpec(
        num_scalar_prefetch=1,
        grid=(db, groups),
        in_specs=[
            pl.BlockSpec((None, ch, LANES), lambda b, j, pt: (b, 0, 0)),
            pl.BlockSpec((1, LANES), lambda b, j, pt: (0, 0)),
            pl.BlockSpec((page, page), lambda b, j, pt: (0, 0)),
            pl.BlockSpec((LANES, ch), lambda b, j, pt: (0, 0)),
        ] + [page_spec(p) for p in range(pages)] * 2,
        out_specs=pl.BlockSpec((None, 1, ch), lambda b, j, pt: (b, 0, 0)),
        scratch_shapes=[pltpu.VMEM((8, ch), F32), pltpu.VMEM((1, LANES), F32)],
    )
    return pl.pallas_call(
        functools.partial(_attn_decode_kernel, pages=pages),
        grid_spec=grid_spec,
        out_shape=jax.ShapeDtypeStruct((db, 1, ch), F32),
        compiler_params=_params("parallel", "arbitrary"),
        name="attn_decode",
    )(page_table, qbd, bias_row, tri, ind, *([cache_k] * pages), *([cache_v] * pages))


def _conv_tail(y, lng, lnb, pww, pwb):
    mu = jnp.mean(y, axis=-1, keepdims=True)
    yc = y - mu
    var = jnp.mean(yc * yc, axis=-1, keepdims=True)
    yn = yc * lax.rsqrt(var + EPS) * lng + lnb
    yn = (yn * jax.nn.sigmoid(yn)).astype(BF16)
    return jnp.dot(yn, pww, preferred_element_type=F32) + pwb


def _pool_window_of_lane(shape):
    lane = lax.broadcasted_iota(jnp.int32, shape, len(shape) - 1)
    gw = shape[-1] // len(POOL_WINDOWS)
    win = jnp.full(shape, POOL_WINDOWS[-1], jnp.int32)
    for g in reversed(range(len(POOL_WINDOWS) - 1)):
        win = jnp.where(lane < (g + 1) * gw, POOL_WINDOWS[g], win)
    return win


HALO = 32


def _mix_prompt_kernel(u_ref, up_ref, h_ref, hp_ref, poolw_ref, pscale_ref, dww_ref, dwb_ref,
                       lng_ref, lnb_ref, pww_ref, pwb_ref, pool_o, conv_o, uext, hext):
    i = pl.program_id(1)
    tt, c = u_ref.shape

    @pl.when(i == 0)
    def _():
        uext[0:HALO, :] = jnp.zeros((HALO, c), F32)
        hext[0:HALO, :] = jnp.zeros((HALO, c), F32)

    @pl.when(i > 0)
    def _():
        uext[0:HALO, :] = up_ref[...]
        hext[0:HALO, :] = hp_ref[...]

    u = u_ref[...]
    uext[HALO:, :] = u
    hext[HALO:, :] = h_ref[...]

    sums = []
    run = u
    k = 1
    for wnd in POOL_WINDOWS:
        while k < wnd:
            run = run + uext[pl.ds(HALO - k, tt), :]
            k += 1
        sums.append(run)
    win = _pool_window_of_lane((tt, c))
    ssel = sums[-1]
    for g in reversed(range(len(POOL_WINDOWS) - 1)):
        ssel = jnp.where(win == POOL_WINDOWS[g], sums[g], ssel)
    pos = i * tt + lax.broadcasted_iota(jnp.int32, (tt, c), 0)
    cnt = jnp.minimum(win, pos + 1).astype(F32)
    d = (ssel / cnt - u).astype(BF16)
    pool_o[...] = (jnp.dot(d, poolw_ref[...], preferred_element_type=F32) * pscale_ref[...]).astype(BF16)

    y = jnp.zeros((tt, c), F32) + dwb_ref[...]
    for tap in range(CONV_TAPS):
        y = y + hext[pl.ds(HALO - CONV_HIST + tap, tt), :] * dww_ref[tap:tap + 1, :]
    conv_o[...] = _conv_tail(y, lng_ref[...], lnb_ref[...], pww_ref[...], pwb_ref[...]).astype(BF16)


def _mix_prompt(u, h, poolw_bd, pscale, dww, dwb, lng, lnb, pww, pwb, *, tt):
    b, t, c = u.shape
    r = tt // HALO
    cur = pl.BlockSpec((None, tt, c), lambda bi, i: (bi, i, 0))
    prev = pl.BlockSpec((None, HALO, c), lambda bi, i: (bi, jnp.maximum(i * r - 1, 0), 0))
    vec = _const_spec((1, c))
    return pl.pallas_call(
        _mix_prompt_kernel,
        grid=(b, t // tt),
        in_specs=[cur, prev, cur, prev, _const_spec((c, c)), vec, _const_spec(dww.shape), vec,
                  vec, vec, _const_spec((c, c)), vec],
        out_specs=[cur, cur],
        out_shape=[jax.ShapeDtypeStruct((b, t, c), BF16)] * 2,
        scratch_shapes=[pltpu.VMEM((HALO + tt, c), F32)] * 2,
        compiler_params=_params("parallel", "arbitrary"),
        name="mix_prompt",
    )(u, u, h, h, poolw_bd, pscale, dww, dwb, lng, lnb, pww, pwb)


def _mix_decode_kernel(u_ref, sp_ref, h_ref, sc_ref, poolw_ref, pscale_ref, dww_ref, dwb_ref,
                       lng_ref, lnb_ref, pww_ref, pwb_ref, pool_o, conv_o, *, start_pos):
    u = u_ref[...]
    sums = []
    run = u
    k = 1
    for wnd in POOL_WINDOWS:
        while k < wnd:
            run = run + sp_ref[POOL_HIST - k]
            k += 1
        sums.append(run)
    win = _pool_window_of_lane(u.shape)
    ssel = sums[-1]
    for g in reversed(range(len(POOL_WINDOWS) - 1)):
        ssel = jnp.where(win == POOL_WINDOWS[g], sums[g], ssel)
    cnt = jnp.minimum(win, start_pos + 1).astype(F32)
    d = (ssel / cnt - u).astype(BF16)
    pool_o[...] = (jnp.dot(d, poolw_ref[...], preferred_element_type=F32) * pscale_ref[...]).astype(BF16)

    y = h_ref[...] * dww_ref[CONV_HIST:CONV_TAPS, :] + dwb_ref[...]
    for tap in range(CONV_HIST):
        y = y + sc_ref[tap] * dww_ref[tap:tap + 1, :]
    conv_o[...] = _conv_tail(y, lng_ref[...], lnb_ref[...], pww_ref[...], pwb_ref[...]).astype(BF16)


def _mix_decode(u, state_pool_tm, h, state_conv_tm, poolw_bd, pscale, dww, dwb, lng, lnb, pww, pwb, *, start_pos):
    db, c = u.shape
    args = (u, state_pool_tm, h, state_conv_tm, poolw_bd, pscale, dww, dwb, lng, lnb, pww, pwb)
    return pl.pallas_call(
        functools.partial(_mix_decode_kernel, start_pos=start_pos),
        grid=(1,),
        in_specs=[_const_spec(a.shape) for a in args],
        out_specs=[_const_spec((db, c))] * 2,
        out_shape=[jax.ShapeDtypeStruct((db, c), BF16)] * 2,
        compiler_params=_params("arbitrary"),
        name="mix_decode",
    )(*args)


def _merge_ffn_kernel(x_ref, att_ref, pool_ref, conv_ref, wo_ref, g_ref, wup_ref, wdn_ref, o_ref, *, ff_chunk):
    ca, cp = att_ref.shape[1], pool_ref.shape[1]
    m = jnp.dot(att_ref[...], wo_ref[0:ca, :], preferred_element_type=F32)
    m = m + jnp.dot(pool_ref[...], wo_ref[ca:ca + cp, :], preferred_element_type=F32)
    m = m + jnp.dot(conv_ref[...], wo_ref[ca + cp:, :], preferred_element_type=F32)
    x1 = x_ref[...] + m
    ms = jnp.mean(x1 * x1, axis=-1, keepdims=True)
    xn = (x1 * lax.rsqrt(ms + EPS) * g_ref[...]).astype(BF16)
    out = x1
    d_ff = wup_ref.shape[1]
    for c0 in range(0, d_ff, ff_chunk):
        hmid = jnp.dot(xn, wup_ref[:, c0:c0 + ff_chunk], preferred_element_type=F32)
        hmid = jnp.square(jnp.maximum(hmid, 0.0)).astype(BF16)
        out = out + jnp.dot(hmid, wdn_ref[c0:c0 + ff_chunk, :], preferred_element_type=F32)
    o_ref[...] = out


def _merge_ffn(x, att, pool, conv, wo_bf, g, wup_bf, wdn_bf, *, tm, ff_chunk):
    m, d = x.shape
    row = lambda c: pl.BlockSpec((tm, c), lambda i: (i, 0))
    single = lambda shape: pl.BlockSpec(shape, lambda i: (0, 0), pipeline_mode=pl.Buffered(1))
    return pl.pallas_call(
        functools.partial(_merge_ffn_kernel, ff_chunk=ff_chunk),
        grid=(m // tm,),
        in_specs=[row(d), row(att.shape[1]), row(pool.shape[1]), row(conv.shape[1]),
                  single(wo_bf.shape), single((1, d)), single(wup_bf.shape), single(wdn_bf.shape)],
        out_specs=row(d),
        out_shape=jax.ShapeDtypeStruct((m, d), F32),
        compiler_params=_params("parallel"),
        name="merge_ffn",
    )(x, att, pool, conv, wo_bf, g, wup_bf, wdn_bf)


def _block_diag(blocks):
    g, a, b = blocks.shape
    eye = jnp.eye(g, dtype=blocks.dtype)
    return (eye[:, None, :, None] * blocks[:, :, None, :]).reshape(g * a, g * b)


def _reverse_tri(n):
    return (jnp.arange(n)[None, :] >= jnp.arange(n)[:, None]).astype(BF16)


ATTN_TQ = 256
PROMPT_TM = 256
MIX_TT = 512
FF_CHUNK = 1024
DECODE_PAGES = 8


def kernel(x_prompt, x_sample, cache_k, cache_v, state_pool, state_conv, page_table, norm_mix_g, w_in, q_norm_g, k_norm_g, sb_bias, pool_w, pool_scale, conv_dw_w, conv_dw_b, conv_ln_g, conv_ln_b, conv_pw_w, conv_pw_b, w_out, norm_ffn_g, w_ffn_up, w_ffn_down):
    b, t, d = x_prompt.shape
    db, ts, _ = x_sample.shape
    assert ts == 1
    depth = w_in.shape[0]
    n_heads = sb_bias.shape[1]
    attn_ch = n_heads * HEAD_DIM
    pool_ch = pool_scale.shape[1]
    conv_ch = conv_dw_b.shape[1]
    past_len = page_table.shape[1] * PAGE_SIZE
    nk = t // ATTN_TQ

    xp = x_prompt.reshape(b * t, d)
    xs = x_sample.reshape(db, d)
    cache_k2 = cache_k.reshape(cache_k.shape[:3] + (attn_ch,))
    cache_v2 = cache_v.reshape(cache_v.shape[:3] + (attn_ch,))

    head_mean = _block_diag(jnp.full((n_heads, HEAD_DIM, HEAD_DIM), 1.0 / HEAD_DIM, BF16))
    tri_prompt = _reverse_tri(ATTN_TQ)
    tri_page = _reverse_tri(PAGE_SIZE)
    head_of_ch = jnp.arange(attn_ch) // HEAD_DIM
    ind = (jnp.arange(LANES)[:, None] == head_of_ch[None, :]).astype(BF16)
    row2 = lambda v: v.reshape(1, -1)

    outs = {n: [] for n in ("kp", "vp", "ks", "vs", "pp", "ps", "cp", "cs")}
    for l in range(depth):
        w_in_bf = w_in[l].astype(BF16)
        qg_t = row2(jnp.tile(q_norm_g[l], n_heads))
        kg_t = row2(jnp.tile(k_norm_g[l], n_heads))
        poolw_bd = _block_diag(pool_w[l]).astype(BF16)
        pww_bf = conv_pw_w[l].astype(BF16)
        mix_w = (poolw_bd, row2(pool_scale[l]), conv_dw_w[l], row2(conv_dw_b[l]),
                 row2(conv_ln_g[l]), row2(conv_ln_b[l]), pww_bf, row2(conv_pw_b[l]))
        inproj = functools.partial(_inproj, g=row2(norm_mix_g[l]), w_bf=w_in_bf, qg_t=qg_t, kg_t=kg_t,
                                   head_mean=head_mean, attn_ch=attn_ch, pool_ch=pool_ch, conv_ch=conv_ch)
        ffn_w = (w_out[l].astype(BF16), row2(norm_ffn_g[l]), w_ffn_up[l].astype(BF16), w_ffn_down[l].astype(BF16))

        qp, kpf, kpb, vpf, vpb, up, hp = inproj(xp, tm=PROMPT_TM)
        qs, ksf, _, vsf, _, us, hs = inproj(xs, tm=db)

        qt = qp.reshape(b, t, n_heads, HEAD_DIM).transpose(0, 2, 3, 1)
        k_hm = kpb.reshape(b, nk, ATTN_TQ, n_heads, HEAD_DIM).transpose(0, 3, 1, 2, 4)
        vt_hm = vpb.reshape(b, nk, ATTN_TQ, n_heads, HEAD_DIM).transpose(0, 3, 1, 4, 2)
        att_t = _attn_prompt(qt, k_hm, vt_hm, tri_prompt, sb_bias[l], tq=ATTN_TQ)
        att_p = att_t.transpose(0, 3, 1, 2).reshape(b * t, attn_ch).astype(BF16)

        qbd = (qs.reshape(db, n_heads, HEAD_DIM)[:, :, :, None]
               * jnp.eye(n_heads, LANES, dtype=BF16)[None, :, None, :]).reshape(db, attn_ch, LANES)
        bias_row = jnp.zeros((1, LANES), F32).at[0, :n_heads].set(sb_bias[l])
        att_s = _attn_decode(page_table, qbd, bias_row, tri_page, ind, cache_k2, cache_v2,
                             layer=l, pages=DECODE_PAGES).reshape(db, attn_ch).astype(BF16)

        pool_p, conv_p = _mix_prompt(up.reshape(b, t, pool_ch), hp.reshape(b, t, conv_ch), *mix_w, tt=MIX_TT)
        pool_s, conv_s = _mix_decode(us, state_pool[l].transpose(1, 0, 2), hs, state_conv[l].transpose(1, 0, 2),
                                     *mix_w, start_pos=past_len)

        xp = _merge_ffn(xp, att_p, pool_p.reshape(b * t, pool_ch), conv_p.reshape(b * t, conv_ch), *ffn_w,
                        tm=PROMPT_TM, ff_chunk=FF_CHUNK)
        xs = _merge_ffn(xs, att_s, pool_s, conv_s, *ffn_w, tm=db, ff_chunk=FF_CHUNK)

        outs["kp"].append(kpf.reshape(b, t, n_heads, HEAD_DIM))
        outs["vp"].append(vpf.reshape(b, t, n_heads, HEAD_DIM))
        outs["ks"].append(ksf.reshape(db, ts, n_heads, HEAD_DIM))
        outs["vs"].append(vsf.reshape(db, ts, n_heads, HEAD_DIM))
        outs["pp"].append(up.reshape(b, t, pool_ch)[:, t - POOL_HIST:])
        outs["ps"].append(jnp.concatenate([state_pool[l][:, 1:], us[:, None, :]], axis=1))
        outs["cp"].append(hp.reshape(b, t, conv_ch)[:, t - CONV_HIST:])
        outs["cs"].append(jnp.concatenate([state_conv[l][:, 1:], hs[:, None, :]], axis=1))

    return (xp.reshape(b, t, d), xs.reshape(db, ts, d),
            jnp.stack(outs["kp"]), jnp.stack(outs["vp"]), jnp.stack(outs["ks"]), jnp.stack(outs["vs"]),
            jnp.stack(outs["pp"]), jnp.stack(outs["ps"]), jnp.stack(outs["cp"]), jnp.stack(outs["cs"]))
```

```python
import functools

import jax
import jax.numpy as jnp
from jax import lax
from jax.experimental import pallas as pl
from jax.experimental.pallas import tpu as pltpu

F32 = jnp.float32
BF16 = jnp.bfloat16

EPS = 1e-6
LOG2E = 1.4426950408889634
HEAD_DIM = 64
POOL_WINDOWS = (2, 4, 8, 16)
POOL_HIST = max(POOL_WINDOWS) - 1
CONV_TAPS = 31
CONV_HIST = CONV_TAPS - 1
PAGE_SIZE = 128

V7X_VMEM_LIMIT_BYTES = 56 * 1024 * 1024
LANES = 128


def _params(*sem):
    return pltpu.CompilerParams(dimension_semantics=sem, vmem_limit_bytes=V7X_VMEM_LIMIT_BYTES)


def _const_spec(shape):
    nd = len(shape)
    return pl.BlockSpec(shape, lambda *_: (0,) * nd)


def _inproj_kernel(x_ref, g_ref, w_ref, qg_ref, kg_ref, hm_ref,
                   q_ref, kf_ref, kb_ref, vf_ref, vb_ref, u_ref, h_ref, *, attn_ch, pool_ch, conv_ch):
    x = x_ref[...]
    ms = jnp.mean(x * x, axis=-1, keepdims=True)
    xn = (x * lax.rsqrt(ms + EPS) * g_ref[...]).astype(BF16)
    z = jnp.dot(xn, w_ref[...], preferred_element_type=F32)
    c0, c1, c2 = attn_ch, 2 * attn_ch, 3 * attn_ch
    c3 = c2 + pool_ch
    c4 = c3 + conv_ch
    q, k, v = z[:, :c0], z[:, c0:c1], z[:, c1:c2]
    u, a, gate = z[:, c2:c3], z[:, c3:c4], z[:, c4:]

    def head_norm(t, gain):
        msq = jnp.dot((t * t).astype(BF16), hm_ref[...], preferred_element_type=F32)
        return t * lax.rsqrt(msq + EPS) * gain

    qn = head_norm(q, qg_ref[...])
    kn = head_norm(k, kg_ref[...])
    q_ref[...] = (qn * (HEAD_DIM ** -0.5 * LOG2E)).astype(BF16)
    kf_ref[...] = kn
    kb_ref[...] = kn.astype(BF16)
    vf_ref[...] = v
    vb_ref[...] = v.astype(BF16)
    u_ref[...] = u
    h_ref[...] = a * jax.nn.sigmoid(gate)


def _inproj(x, g, w_bf, qg_t, kg_t, head_mean, *, tm, attn_ch, pool_ch, conv_ch):
    m, d = x.shape
    ncol = w_bf.shape[1]
    row = lambda c: pl.BlockSpec((tm, c), lambda i: (i, 0))
    outs = [
        jax.ShapeDtypeStruct((m, attn_ch), BF16),
        jax.ShapeDtypeStruct((m, attn_ch), F32),
        jax.ShapeDtypeStruct((m, attn_ch), BF16),
        jax.ShapeDtypeStruct((m, attn_ch), F32),
        jax.ShapeDtypeStruct((m, attn_ch), BF16),
        jax.ShapeDtypeStruct((m, pool_ch), F32),
        jax.ShapeDtypeStruct((m, conv_ch), F32),
    ]
    return pl.pallas_call(
        functools.partial(_inproj_kernel, attn_ch=attn_ch, pool_ch=pool_ch, conv_ch=conv_ch),
        grid=(m // tm,),
        in_specs=[row(d), _const_spec((1, d)), _const_spec((d, ncol)),
                  _const_spec((1, attn_ch)), _const_spec((1, attn_ch)), _const_spec((attn_ch, attn_ch))],
        out_specs=[row(attn_ch)] * 5 + [row(pool_ch), row(conv_ch)],
        out_shape=outs,
        compiler_params=_params("parallel"),
        name="inproj",
    )(x, g, w_bf, qg_t, kg_t, head_mean)


SOFTPLUS2_CLAMP = 100.0


def _softplus2(z2):
    return jnp.maximum(z2, jnp.log(1.0 + jnp.exp2(jnp.minimum(z2, SOFTPLUS2_CLAMP))) * LOG2E)


def _attn_prompt_kernel(qt_ref, k_ref, vt_ref, tri_ref, o_ref, r_ref, za_ref, zb_ref):
    i = pl.program_id(2)
    heads = qt_ref.shape[0]
    tri = tri_ref[...]
    tk, tq = tri.shape[0], qt_ref.shape[-1]

    def scores(j, z_ref):
        for g in range(heads):
            z_ref[g] = jnp.dot(k_ref[g, j], qt_ref[g], preferred_element_type=F32)

    def step(j, zc_ref, zn_ref, diagonal=False):
        scores(jnp.maximum(j - 1, 0), zn_ref)
        if diagonal:
            vis = lax.broadcasted_iota(jnp.int32, (tk, tq), 0) < lax.broadcasted_iota(jnp.int32, (tk, tq), 1)
            sps = [jnp.where(vis, _softplus2(zc_ref[g]), 0.0) for g in range(heads)]
        else:
            sps = [_softplus2(zc_ref[g]) for g in range(heads)]
        ss = [jnp.dot(tri, sp.astype(BF16), preferred_element_type=F32) for sp in sps]
        for g in range(heads):
            w = jnp.exp2(zc_ref[g] - ss[g])
            if diagonal:
                w = jnp.where(vis, w, 0.0)
            pv = jnp.dot(vt_ref[g, j], w.astype(BF16), preferred_element_type=F32)
            if diagonal:
                o_ref[g] = pv
                r_ref[g] = ss[g][0:1, :]
            else:
                r = r_ref[g]
                o_ref[g] += pv * jnp.exp2(-r)
                r_ref[g] = r + ss[g][0:1, :]

    scores(i, za_ref)
    step(i, za_ref, zb_ref, diagonal=True)

    def body(p, carry):
        j = i - 1 - 2 * p
        step(j, zb_ref, za_ref)
        step(j - 1, za_ref, zb_ref)
        return carry

    lax.fori_loop(0, i // 2, body, 0)

    @pl.when(i % 2 == 1)
    def _():
        step(0, zb_ref, za_ref)


def _attn_prompt(qt, k_hm, vt_hm, tri, *, tq, heads):
    b, h, da, t = qt.shape
    nk, tk = k_hm.shape[2], k_hm.shape[3]
    dh = vt_hm.shape[3]
    return pl.pallas_call(
        _attn_prompt_kernel,
        grid=(b, h // heads, t // tq),
        in_specs=[
            pl.BlockSpec((None, heads, da, tq), lambda bi, hi, i: (bi, hi, 0, i)),
            pl.BlockSpec((None, heads, nk, tk, da), lambda bi, hi, i: (bi, hi, 0, 0, 0)),
            pl.BlockSpec((None, heads, nk, dh, tk), lambda bi, hi, i: (bi, hi, 0, 0, 0)),
            _const_spec((tk, tk)),
        ],
        out_specs=pl.BlockSpec((None, heads, dh, tq), lambda bi, hi, i: (bi, hi, 0, i)),
        out_shape=jax.ShapeDtypeStruct((b, h, dh, t), F32),
        scratch_shapes=[pltpu.VMEM((heads, 1, tq), F32)] + [pltpu.VMEM((heads, tk, tq), F32)] * 2,
        compiler_params=_params("parallel", "parallel", "arbitrary"),
        name="attn_prompt",
    )(qt, k_hm, vt_hm, tri)


def _attn_decode_kernel(pt_ref, q_ref, bias_ref, *refs, pages):
    del pt_ref
    k_refs, v_refs = refs[:pages], refs[pages:2 * pages]
    o_ref, run_ref = refs[2 * pages:]
    j = pl.program_id(1)

    @pl.when(j == 0)
    def _():
        o_ref[...] = jnp.zeros_like(o_ref)
        run_ref[...] = jnp.zeros_like(run_ref)

    q = q_ref[...]
    bias = bias_ref[...]
    run = run_ref[...]
    acc = o_ref[...]
    for p in reversed(range(pages)):
        k = k_refs[p][...]
        z = jnp.sum(k * q, axis=-1, keepdims=True) + bias
        sp = _softplus2(z)
        sums = []
        for s in reversed(range(k.shape[0])):
            run = run + sp[s]
            sums.append(run)
        w = jnp.exp2(z - jnp.stack(sums[::-1], axis=0))
        acc = acc + jnp.sum(w * v_refs[p][...], axis=0)
    o_ref[...] = acc
    run_ref[...] = run


def _attn_decode(page_table, q, bias_col, cache_k, cache_v, *, layer, pages):
    db, n_pages = page_table.shape
    page, nh, dh = cache_k.shape[2:]
    groups = n_pages // pages

    def page_spec(p):
        return pl.BlockSpec(
            (None, None, page, nh, dh),
            lambda b, j, pt: (layer, pt[b, (groups - 1 - j) * pages + p], 0, 0, 0))

    grid_spec = pltpu.PrefetchScalarGridSpec(
        num_scalar_prefetch=1,
        grid=(db, groups),
        in_specs=[
            pl.BlockSpec((None, nh, dh), lambda b, j, pt: (b, 0, 0)),
            pl.BlockSpec((nh, 1), lambda b, j, pt: (0, 0)),
        ] + [page_spec(p) for p in range(pages)] * 2,
        out_specs=pl.BlockSpec((None, nh, dh), lambda b, j, pt: (b, 0, 0)),
        scratch_shapes=[pltpu.VMEM((nh, 1), F32)],
    )
    return pl.pallas_call(
        functools.partial(_attn_decode_kernel, pages=pages),
        grid_spec=grid_spec,
        out_shape=jax.ShapeDtypeStruct((db, nh, dh), F32),
        compiler_params=_params("parallel", "arbitrary"),
        name="attn_decode",
    )(page_table, q, bias_col, *([cache_k] * pages), *([cache_v] * pages))


def _conv_tail(y, lng, lnb, pww, pwb):
    mu = jnp.mean(y, axis=-1, keepdims=True)
    yc = y - mu
    var = jnp.mean(yc * yc, axis=-1, keepdims=True)
    yn = yc * lax.rsqrt(var + EPS) * lng + lnb
    yn = (yn * jax.nn.sigmoid(yn)).astype(BF16)
    return jnp.dot(yn, pww, preferred_element_type=F32) + pwb


def _pool_window_of_lane(shape):
    lane = lax.broadcasted_iota(jnp.int32, shape, len(shape) - 1)
    gw = shape[-1] // len(POOL_WINDOWS)
    win = jnp.full(shape, POOL_WINDOWS[-1], jnp.int32)
    for g in reversed(range(len(POOL_WINDOWS) - 1)):
        win = jnp.where(lane < (g + 1) * gw, POOL_WINDOWS[g], win)
    return win


HALO = 32


def _mix_prompt_kernel(u_ref, up_ref, h_ref, hp_ref, poolw_ref, pscale_ref, dww_ref, dwb_ref,
                       lng_ref, lnb_ref, pww_ref, pwb_ref, pool_o, conv_o, uext, hext):
    i = pl.program_id(1)
    tt, c = u_ref.shape

    @pl.when(i == 0)
    def _():
        uext[0:HALO, :] = jnp.zeros((HALO, c), F32)
        hext[0:HALO, :] = jnp.zeros((HALO, c), F32)

    @pl.when(i > 0)
    def _():
        uext[0:HALO, :] = up_ref[...]
        hext[0:HALO, :] = hp_ref[...]

    u = u_ref[...]
    uext[HALO:, :] = u
    hext[HALO:, :] = h_ref[...]

    sums = []
    run = u
    k = 1
    for wnd in POOL_WINDOWS:
        while k < wnd:
            run = run + uext[pl.ds(HALO - k, tt), :]
            k += 1
        sums.append(run)
    win = _pool_window_of_lane((tt, c))
    ssel = sums[-1]
    for g in reversed(range(len(POOL_WINDOWS) - 1)):
        ssel = jnp.where(win == POOL_WINDOWS[g], sums[g], ssel)
    pos = i * tt + lax.broadcasted_iota(jnp.int32, (tt, c), 0)
    cnt = jnp.minimum(win, pos + 1).astype(F32)
    d = (ssel / cnt - u).astype(BF16)
    pool_o[...] = (jnp.dot(d, poolw_ref[...], preferred_element_type=F32) * pscale_ref[...]).astype(BF16)

    y = jnp.zeros((tt, c), F32) + dwb_ref[...]
    for tap in range(CONV_TAPS):
        y = y + hext[pl.ds(HALO - CONV_HIST + tap, tt), :] * dww_ref[tap:tap + 1, :]
    conv_o[...] = _conv_tail(y, lng_ref[...], lnb_ref[...], pww_ref[...], pwb_ref[...]).astype(BF16)


def _mix_prompt(u, h, poolw_bd, pscale, dww, dwb, lng, lnb, pww, pwb, *, tt):
    b, t, c = u.shape
    r = tt // HALO
    cur = pl.BlockSpec((None, tt, c), lambda bi, i: (bi, i, 0))
    prev = pl.BlockSpec((None, HALO, c), lambda bi, i: (bi, jnp.maximum(i * r - 1, 0), 0))
    vec = _const_spec((1, c))
    return pl.pallas_call(
        _mix_prompt_kernel,
        grid=(b, t // tt),
        in_specs=[cur, prev, cur, prev, _const_spec((c, c)), vec, _const_spec(dww.shape), vec,
                  vec, vec, _const_spec((c, c)), vec],
        out_specs=[cur, cur],
        out_shape=[jax.ShapeDtypeStruct((b, t, c), BF16)] * 2,
        scratch_shapes=[pltpu.VMEM((HALO + tt, c), F32)] * 2,
        compiler_params=_params("parallel", "arbitrary"),
        name="mix_prompt",
    )(u, u, h, h, poolw_bd, pscale, dww, dwb, lng, lnb, pww, pwb)


def _mix_decode_kernel(u_ref, sp_ref, h_ref, sc_ref, poolw_ref, pscale_ref, dww_ref, dwb_ref,
                       lng_ref, lnb_ref, pww_ref, pwb_ref, pool_o, conv_o, *, start_pos):
    u = u_ref[...]
    sums = []
    run = u
    k = 1
    for wnd in POOL_WINDOWS:
        while k < wnd:
            run = run + sp_ref[POOL_HIST - k]
            k += 1
        sums.append(run)
    win = _pool_window_of_lane(u.shape)
    ssel = sums[-1]
    for g in reversed(range(len(POOL_WINDOWS) - 1)):
        ssel = jnp.where(win == POOL_WINDOWS[g], sums[g], ssel)
    cnt = jnp.minimum(win, start_pos + 1).astype(F32)
    d = (ssel / cnt - u).astype(BF16)
    pool_o[...] = (jnp.dot(d, poolw_ref[...], preferred_element_type=F32) * pscale_ref[...]).astype(BF16)

    y = h_ref[...] * dww_ref[CONV_HIST:CONV_TAPS, :] + dwb_ref[...]
    for tap in range(CONV_HIST):
        y = y + sc_ref[tap] * dww_ref[tap:tap + 1, :]
    conv_o[...] = _conv_tail(y, lng_ref[...], lnb_ref[...], pww_ref[...], pwb_ref[...]).astype(BF16)


def _mix_decode(u, state_pool_tm, h, state_conv_tm, poolw_bd, pscale, dww, dwb, lng, lnb, pww, pwb, *, start_pos):
    db, c = u.shape
    args = (u, state_pool_tm, h, state_conv_tm, poolw_bd, pscale, dww, dwb, lng, lnb, pww, pwb)
    return pl.pallas_call(
        functools.partial(_mix_decode_kernel, start_pos=start_pos),
        grid=(1,),
        in_specs=[_const_spec(a.shape) for a in args],
        out_specs=[_const_spec((db, c))] * 2,
        out_shape=[jax.ShapeDtypeStruct((db, c), BF16)] * 2,
        compiler_params=_params("arbitrary"),
        name="mix_decode",
    )(*args)


def _merge_ffn_kernel(x_ref, att_ref, pool_ref, conv_ref, wo_ref, g_ref, wup_ref, wdn_ref, o_ref, *, ff_chunk):
    ca, cp = att_ref.shape[1], pool_ref.shape[1]
    m = jnp.dot(att_ref[...], wo_ref[0:ca, :], preferred_element_type=F32)
    m = m + jnp.dot(pool_ref[...], wo_ref[ca:ca + cp, :], preferred_element_type=F32)
    m = m + jnp.dot(conv_ref[...], wo_ref[ca + cp:, :], preferred_element_type=F32)
    x1 = x_ref[...] + m
    ms = jnp.mean(x1 * x1, axis=-1, keepdims=True)
    xn = (x1 * lax.rsqrt(ms + EPS) * g_ref[...]).astype(BF16)
    out = x1
    d_ff = wup_ref.shape[1]
    for c0 in range(0, d_ff, ff_chunk):
        hmid = jnp.dot(xn, wup_ref[:, c0:c0 + ff_chunk], preferred_element_type=F32)
        hmid = jnp.square(jnp.maximum(hmid, 0.0)).astype(BF16)
        out = out + jnp.dot(hmid, wdn_ref[c0:c0 + ff_chunk, :], preferred_element_type=F32)
    o_ref[...] = out


def _merge_ffn(x, att, pool, conv, wo_bf, g, wup_bf, wdn_bf, *, tm, ff_chunk):
    m, d = x.shape
    row = lambda c: pl.BlockSpec((tm, c), lambda i: (i, 0))
    single = lambda shape: pl.BlockSpec(shape, lambda i: (0, 0), pipeline_mode=pl.Buffered(1))
    return pl.pallas_call(
        functools.partial(_merge_ffn_kernel, ff_chunk=ff_chunk),
        grid=(m // tm,),
        in_specs=[row(d), row(att.shape[1]), row(pool.shape[1]), row(conv.shape[1]),
                  single(wo_bf.shape), single((1, d)), single(wup_bf.shape), single(wdn_bf.shape)],
        out_specs=row(d),
        out_shape=jax.ShapeDtypeStruct((m, d), F32),
        compiler_params=_params("parallel"),
        name="merge_ffn",
    )(x, att, pool, conv, wo_bf, g, wup_bf, wdn_bf)


def _block_diag(blocks):
    g, a, b = blocks.shape
    eye = jnp.eye(g, dtype=blocks.dtype)
    return (eye[:, None, :, None] * blocks[:, :, None, :]).reshape(g * a, g * b)


def _reverse_tri(n):
    return (jnp.arange(n)[None, :] >= jnp.arange(n)[:, None]).astype(BF16)


ATTN_TQ = 256
ATTN_HEADS_PER_STEP = 4
PROMPT_TM = 256
MIX_TT = 512
FF_CHUNK = 1024
DECODE_PAGES = 8


def kernel(x_prompt, x_sample, cache_k, cache_v, state_pool, state_conv, page_table, norm_mix_g, w_in, q_norm_g, k_norm_g, sb_bias, pool_w, pool_scale, conv_dw_w, conv_dw_b, conv_ln_g, conv_ln_b, conv_pw_w, conv_pw_b, w_out, norm_ffn_g, w_ffn_up, w_ffn_down):
    b, t, d = x_prompt.shape
    db, ts, _ = x_sample.shape
    assert ts == 1
    depth = w_in.shape[0]
    n_heads = sb_bias.shape[1]
    attn_ch = n_heads * HEAD_DIM
    pool_ch = pool_scale.shape[1]
    conv_ch = conv_dw_b.shape[1]
    past_len = page_table.shape[1] * PAGE_SIZE
    nk = t // ATTN_TQ

    xp = x_prompt.reshape(b * t, d)
    xs = x_sample.reshape(db, d)

    head_mean = _block_diag(jnp.full((n_heads, HEAD_DIM, HEAD_DIM), 1.0 / HEAD_DIM, BF16))
    tri_prompt = _reverse_tri(ATTN_TQ)
    q_pad = jnp.zeros((b, n_heads, HEAD_DIM, t), BF16).at[:, :, 0:2, :].set(1.0)
    row2 = lambda v: v.reshape(1, -1)

    outs = {n: [] for n in ("kp", "vp", "ks", "vs", "pp", "ps", "cp", "cs")}
    for l in range(depth):
        w_in_bf = w_in[l].astype(BF16)
        qg_t = row2(jnp.tile(q_norm_g[l], n_heads))
        kg_t = row2(jnp.tile(k_norm_g[l], n_heads))
        poolw_bd = _block_diag(pool_w[l]).astype(BF16)
        pww_bf = conv_pw_w[l].astype(BF16)
        mix_w = (poolw_bd, row2(pool_scale[l]), conv_dw_w[l], row2(conv_dw_b[l]),
                 row2(conv_ln_g[l]), row2(conv_ln_b[l]), pww_bf, row2(conv_pw_b[l]))
        inproj = functools.partial(_inproj, g=row2(norm_mix_g[l]), w_bf=w_in_bf, qg_t=qg_t, kg_t=kg_t,
                                   head_mean=head_mean, attn_ch=attn_ch, pool_ch=pool_ch, conv_ch=conv_ch)
        ffn_w = (w_out[l].astype(BF16), row2(norm_ffn_g[l]), w_ffn_up[l].astype(BF16), w_ffn_down[l].astype(BF16))
        bias2 = sb_bias[l] * LOG2E
        bias_hi = bias2.astype(BF16)
        bias_lo = (bias2 - bias_hi.astype(F32)).astype(BF16)

        qp, kpf, kpb, vpf, vpb, up, hp = inproj(xp, tm=PROMPT_TM)
        qs, ksf, _, vsf, _, us, hs = inproj(xs, tm=db)

        qt = jnp.concatenate([qp.reshape(b, t, n_heads, HEAD_DIM).transpose(0, 2, 3, 1), q_pad], axis=2)
        k_pad = jnp.zeros((n_heads, HEAD_DIM), BF16).at[:, 0].set(bias_hi).at[:, 1].set(bias_lo)
        k_hm = kpb.reshape(b, nk, ATTN_TQ, n_heads, HEAD_DIM).transpose(0, 3, 1, 2, 4)
        k_hm = jnp.concatenate(
            [k_hm, jnp.broadcast_to(k_pad[None, :, None, None, :], k_hm.shape)], axis=-1)
        vt_hm = vpb.reshape(b, nk, ATTN_TQ, n_heads, HEAD_DIM).transpose(0, 3, 1, 4, 2)
        att_t = _attn_prompt(qt, k_hm, vt_hm, tri_prompt, tq=ATTN_TQ, heads=ATTN_HEADS_PER_STEP)
        att_p = att_t.transpose(0, 3, 1, 2).reshape(b * t, attn_ch).astype(BF16)

        att_s = _attn_decode(page_table, qs.astype(F32).reshape(db, n_heads, HEAD_DIM), bias2.reshape(n_heads, 1),
                             cache_k, cache_v, layer=l, pages=DECODE_PAGES).reshape(db, attn_ch).astype(BF16)

        pool_p, conv_p = _mix_prompt(up.reshape(b, t, pool_ch), hp.reshape(b, t, conv_ch), *mix_w, tt=MIX_TT)
        pool_s, conv_s = _mix_decode(us, state_pool[l].transpose(1, 0, 2), hs, state_conv[l].transpose(1, 0, 2),
                                     *mix_w, start_pos=past_len)

        xp = _merge_ffn(xp, att_p, pool_p.reshape(b * t, pool_ch), conv_p.reshape(b * t, conv_ch), *ffn_w,
                        tm=PROMPT_TM, ff_chunk=FF_CHUNK)
        xs = _merge_ffn(xs, att_s, pool_s, conv_s, *ffn_w, tm=db, ff_chunk=FF_CHUNK)

        outs["kp"].append(kpf.reshape(b, t, n_heads, HEAD_DIM))
        outs["vp"].append(vpf.reshape(b, t, n_heads, HEAD_DIM))
        outs["ks"].append(ksf.reshape(db, ts, n_heads, HEAD_DIM))
        outs["vs"].append(vsf.reshape(db, ts, n_heads, HEAD_DIM))
        outs["pp"].append(up.reshape(b, t, pool_ch)[:, t - POOL_HIST:])
        outs["ps"].append(jnp.concatenate([state_pool[l][:, 1:], us[:, None, :]], axis=1))
        outs["cp"].append(hp.reshape(b, t, conv_ch)[:, t - CONV_HIST:])
        outs["cs"].append(jnp.concatenate([state_conv[l][:, 1:], hs[:, None, :]], axis=1))

    return (xp.reshape(b, t, d), xs.reshape(db, ts, d),
            jnp.stack(outs["kp"]), jnp.stack(outs["vp"]), jnp.stack(outs["ks"]), jnp.stack(outs["vs"]),
            jnp.stack(outs["pp"]), jnp.stack(outs["ps"]), jnp.stack(outs["cp"]), jnp.stack(outs["cs"]))
```

```python
import functools

import jax
import jax.numpy as jnp
from jax import lax
from jax.experimental import pallas as pl
from jax.experimental.pallas import tpu as pltpu

F32 = jnp.float32
BF16 = jnp.bfloat16

EPS = 1e-6
LOG2E = 1.4426950408889634
HEAD_DIM = 64
POOL_WINDOWS = (2, 4, 8, 16)
POOL_HIST = max(POOL_WINDOWS) - 1
CONV_TAPS = 31
CONV_HIST = CONV_TAPS - 1
PAGE_SIZE = 128

V7X_VMEM_LIMIT_BYTES = 56 * 1024 * 1024
LANES = 128


def _params(*sem):
    return pltpu.CompilerParams(dimension_semantics=sem, vmem_limit_bytes=V7X_VMEM_LIMIT_BYTES)


def _const_spec(shape):
    nd = len(shape)
    return pl.BlockSpec(shape, lambda *_: (0,) * nd)


def _inproj_kernel(x_ref, g_ref, w_ref, qg_ref, kg_ref, hm_ref,
                   q_ref, kf_ref, kb_ref, vf_ref, vb_ref, u_ref, h_ref, *, attn_ch, pool_ch, conv_ch):
    x = x_ref[...]
    ms = jnp.mean(x * x, axis=-1, keepdims=True)
    xn = (x * lax.rsqrt(ms + EPS) * g_ref[...]).astype(BF16)
    z = jnp.dot(xn, w_ref[...], preferred_element_type=F32)
    c0, c1, c2 = attn_ch, 2 * attn_ch, 3 * attn_ch
    c3 = c2 + pool_ch
    c4 = c3 + conv_ch
    q, k, v = z[:, :c0], z[:, c0:c1], z[:, c1:c2]
    u, a, gate = z[:, c2:c3], z[:, c3:c4], z[:, c4:]

    def head_norm(t, gain):
        msq = jnp.dot((t * t).astype(BF16), hm_ref[...], preferred_element_type=F32)
        return t * lax.rsqrt(msq + EPS) * gain

    qn = head_norm(q, qg_ref[...])
    kn = head_norm(k, kg_ref[...])
    q_ref[...] = (qn * (HEAD_DIM ** -0.5 * LOG2E)).astype(BF16)
    kf_ref[...] = kn
    kb_ref[...] = kn.astype(BF16)
    vf_ref[...] = v
    vb_ref[...] = v.astype(BF16)
    u_ref[...] = u
    h_ref[...] = a * jax.nn.sigmoid(gate)


def _inproj(x, g, w_bf, qg_t, kg_t, head_mean, *, tm, attn_ch, pool_ch, conv_ch):
    m, d = x.shape
    ncol = w_bf.shape[1]
    row = lambda c: pl.BlockSpec((tm, c), lambda i: (i, 0))
    outs = [
        jax.ShapeDtypeStruct((m, attn_ch), BF16),
        jax.ShapeDtypeStruct((m, attn_ch), F32),
        jax.ShapeDtypeStruct((m, attn_ch), BF16),
        jax.ShapeDtypeStruct((m, attn_ch), F32),
        jax.ShapeDtypeStruct((m, attn_ch), BF16),
        jax.ShapeDtypeStruct((m, pool_ch), F32),
        jax.ShapeDtypeStruct((m, conv_ch), F32),
    ]
    return pl.pallas_call(
        functools.partial(_inproj_kernel, attn_ch=attn_ch, pool_ch=pool_ch, conv_ch=conv_ch),
        grid=(m // tm,),
        in_specs=[row(d), _const_spec((1, d)), _const_spec((d, ncol)),
                  _const_spec((1, attn_ch)), _const_spec((1, attn_ch)), _const_spec((attn_ch, attn_ch))],
        out_specs=[row(attn_ch)] * 5 + [row(pool_ch), row(conv_ch)],
        out_shape=outs,
        compiler_params=_params("parallel"),
        name="inproj",
    )(x, g, w_bf, qg_t, kg_t, head_mean)


SOFTPLUS2_CLAMP = 100.0


def _softplus2(z2):
    return jnp.maximum(z2, jnp.log(1.0 + jnp.exp2(jnp.minimum(z2, SOFTPLUS2_CLAMP))) * LOG2E)


def _attn_prompt_kernel(qt_ref, k_ref, vt_ref, tri_ref, o_ref, r_ref, za_ref, zb_ref):
    i = pl.program_id(2)
    heads = qt_ref.shape[0]
    tri = tri_ref[...]
    tk, tq = tri.shape[0], qt_ref.shape[-1]

    def scores(j, z_ref):
        for g in range(heads):
            z_ref[g] = jnp.dot(k_ref[g, j], qt_ref[g], preferred_element_type=F32)

    def step(j, zc_ref, zn_ref, diagonal=False):
        scores(jnp.maximum(j - 1, 0), zn_ref)
        if diagonal:
            vis = lax.broadcasted_iota(jnp.int32, (tk, tq), 0) < lax.broadcasted_iota(jnp.int32, (tk, tq), 1)
            sps = [jnp.where(vis, _softplus2(zc_ref[g]), 0.0) for g in range(heads)]
        else:
            sps = [_softplus2(zc_ref[g]) for g in range(heads)]
        ss = [jnp.dot(tri, sp.astype(BF16), preferred_element_type=F32) for sp in sps]
        for g in range(heads):
            w = jnp.exp2(zc_ref[g] - ss[g])
            if diagonal:
                w = jnp.where(vis, w, 0.0)
            pv = jnp.dot(vt_ref[g, j], w.astype(BF16), preferred_element_type=F32)
            if diagonal:
                o_ref[g] = pv
                r_ref[g] = ss[g][0:1, :]
            else:
                r = r_ref[g]
                o_ref[g] += pv * jnp.exp2(-r)
                r_ref[g] = r + ss[g][0:1, :]

    scores(i, za_ref)
    step(i, za_ref, zb_ref, diagonal=True)

    def body(p, carry):
        j = i - 1 - 2 * p
        step(j, zb_ref, za_ref)
        step(j - 1, za_ref, zb_ref)
        return carry

    lax.fori_loop(0, i // 2, body, 0)

    @pl.when(i % 2 == 1)
    def _():
        step(0, zb_ref, za_ref)


def _attn_prompt(qt, k_hm, vt_hm, tri, *, tq, heads):
    b, h, da, t = qt.shape
    nk, tk = k_hm.shape[2], k_hm.shape[3]
    dh = vt_hm.shape[3]
    return pl.pallas_call(
        _attn_prompt_kernel,
        grid=(b, h // heads, t // tq),
        in_specs=[
            pl.BlockSpec((None, heads, da, tq), lambda bi, hi, i: (bi, hi, 0, i)),
            pl.BlockSpec((None, heads, nk, tk, da), lambda bi, hi, i: (bi, hi, 0, 0, 0)),
            pl.BlockSpec((None, heads, nk, dh, tk), lambda bi, hi, i: (bi, hi, 0, 0, 0)),
            _const_spec((tk, tk)),
        ],
        out_specs=pl.BlockSpec((None, heads, dh, tq), lambda bi, hi, i: (bi, hi, 0, i)),
        out_shape=jax.ShapeDtypeStruct((b, h, dh, t), F32),
        scratch_shapes=[pltpu.VMEM((heads, 1, tq), F32)] + [pltpu.VMEM((heads, tk, tq), F32)] * 2,
        compiler_params=_params("parallel", "parallel", "arbitrary"),
        name="attn_prompt",
    )(qt, k_hm, vt_hm, tri)


def _attn_decode_kernel(pt_ref, q_ref, bias_ref, tri_ref, *refs, pages):
    del pt_ref
    k_refs, v_refs = refs[:pages], refs[pages:2 * pages]
    o_ref, acc_ref, r_ref = refs[2 * pages:]
    j = pl.program_id(1)

    @pl.when(j == 0)
    def _():
        acc_ref[...] = jnp.zeros_like(acc_ref)
        r_ref[...] = jnp.zeros_like(r_ref)

    q = q_ref[...]
    bias = bias_ref[...]
    tri = tri_ref[...]
    r = r_ref[...]
    for p in reversed(range(pages)):
        z = jnp.sum(k_refs[p][...] * q, axis=1) + bias
        s = jnp.dot(_softplus2(z).astype(BF16), tri, preferred_element_type=F32)
        w = jnp.exp2(z - s - r)
        acc_ref[...] += w[:, None, :] * v_refs[p][...]
        r = r + s[:, 0:1]
    r_ref[...] = r

    @pl.when(j == pl.num_programs(1) - 1)
    def _():
        o_ref[...] = jnp.sum(acc_ref[...], axis=-1)


def _attn_decode(page_table, q_rep, bias_rep, tri, cache_kt, cache_vt, *, layer, pages):
    db, n_pages = page_table.shape
    nh, dh, page = cache_kt.shape[2:]
    groups = n_pages // pages

    def page_spec(p):
        return pl.BlockSpec(
            (None, None, nh, dh, page),
            lambda b, j, pt: (layer, pt[b, (groups - 1 - j) * pages + p], 0, 0, 0))

    grid_spec = pltpu.PrefetchScalarGridSpec(
        num_scalar_prefetch=1,
        grid=(db, groups),
        in_specs=[
            pl.BlockSpec((None, nh, dh, page), lambda b, j, pt: (b, 0, 0, 0)),
            pl.BlockSpec((nh, page), lambda b, j, pt: (0, 0)),
            pl.BlockSpec((page, page), lambda b, j, pt: (0, 0)),
        ] + [page_spec(p) for p in range(pages)] * 2,
        out_specs=pl.BlockSpec((None, nh, dh), lambda b, j, pt: (b, 0, 0)),
        scratch_shapes=[pltpu.VMEM((nh, dh, page), F32), pltpu.VMEM((nh, 1), F32)],
    )
    return pl.pallas_call(
        functools.partial(_attn_decode_kernel, pages=pages),
        grid_spec=grid_spec,
        out_shape=jax.ShapeDtypeStruct((db, nh, dh), F32),
        compiler_params=_params("parallel", "arbitrary"),
        name="attn_decode",
    )(page_table, q_rep, bias_rep, tri, *([cache_kt] * pages), *([cache_vt] * pages))


def _conv_tail(y, lng, lnb, pww, pwb):
    mu = jnp.mean(y, axis=-1, keepdims=True)
    yc = y - mu
    var = jnp.mean(yc * yc, axis=-1, keepdims=True)
    yn = yc * lax.rsqrt(var + EPS) * lng + lnb
    yn = (yn * jax.nn.sigmoid(yn)).astype(BF16)
    return jnp.dot(yn, pww, preferred_element_type=F32) + pwb


def _pool_window_of_lane(shape):
    lane = lax.broadcasted_iota(jnp.int32, shape, len(shape) - 1)
    gw = shape[-1] // len(POOL_WINDOWS)
    win = jnp.full(shape, POOL_WINDOWS[-1], jnp.int32)
    for g in reversed(range(len(POOL_WINDOWS) - 1)):
        win = jnp.where(lane < (g + 1) * gw, POOL_WINDOWS[g], win)
    return win


HALO = 32


def _mix_prompt_kernel(u_ref, up_ref, h_ref, hp_ref, poolw_ref, pscale_ref, dww_ref, dwb_ref,
                       lng_ref, lnb_ref, pww_ref, pwb_ref, pool_o, conv_o, uext, hext):
    i = pl.program_id(1)
    tt, c = u_ref.shape

    @pl.when(i == 0)
    def _():
        uext[0:HALO, :] = jnp.zeros((HALO, c), F32)
        hext[0:HALO, :] = jnp.zeros((HALO, c), F32)

    @pl.when(i > 0)
    def _():
        uext[0:HALO, :] = up_ref[...]
        hext[0:HALO, :] = hp_ref[...]

    u = u_ref[...]
    uext[HALO:, :] = u
    hext[HALO:, :] = h_ref[...]

    sums = []
    run = u
    k = 1
    for wnd in POOL_WINDOWS:
        while k < wnd:
            run = run + uext[pl.ds(HALO - k, tt), :]
            k += 1
        sums.append(run)
    win = _pool_window_of_lane((tt, c))
    ssel = sums[-1]
    for g in reversed(range(len(POOL_WINDOWS) - 1)):
        ssel = jnp.where(win == POOL_WINDOWS[g], sums[g], ssel)
    pos = i * tt + lax.broadcasted_iota(jnp.int32, (tt, c), 0)
    cnt = jnp.minimum(win, pos + 1).astype(F32)
    d = (ssel / cnt - u).astype(BF16)
    pool_o[...] = (jnp.dot(d, poolw_ref[...], preferred_element_type=F32) * pscale_ref[...]).astype(BF16)

    y = jnp.zeros((tt, c), F32) + dwb_ref[...]
    for tap in range(CONV_TAPS):
        y = y + hext[pl.ds(HALO - CONV_HIST + tap, tt), :] * dww_ref[tap:tap + 1, :]
    conv_o[...] = _conv_tail(y, lng_ref[...], lnb_ref[...], pww_ref[...], pwb_ref[...]).astype(BF16)


def _mix_prompt(u, h, poolw_bd, pscale, dww, dwb, lng, lnb, pww, pwb, *, tt):
    b, t, c = u.shape
    r = tt // HALO
    cur = pl.BlockSpec((None, tt, c), lambda bi, i: (bi, i, 0))
    prev = pl.BlockSpec((None, HALO, c), lambda bi, i: (bi, jnp.maximum(i * r - 1, 0), 0))
    vec = _const_spec((1, c))
    return pl.pallas_call(
        _mix_prompt_kernel,
        grid=(b, t // tt),
        in_specs=[cur, prev, cur, prev, _const_spec((c, c)), vec, _const_spec(dww.shape), vec,
                  vec, vec, _const_spec((c, c)), vec],
        out_specs=[cur, cur],
        out_shape=[jax.ShapeDtypeStruct((b, t, c), BF16)] * 2,
        scratch_shapes=[pltpu.VMEM((HALO + tt, c), F32)] * 2,
        compiler_params=_params("parallel", "arbitrary"),
        name="mix_prompt",
    )(u, u, h, h, poolw_bd, pscale, dww, dwb, lng, lnb, pww, pwb)


def _mix_decode_kernel(u_ref, sp_ref, h_ref, sc_ref, poolw_ref, pscale_ref, dww_ref, dwb_ref,
                       lng_ref, lnb_ref, pww_ref, pwb_ref, pool_o, conv_o, *, start_pos):
    u = u_ref[...]
    sums = []
    run = u
    k = 1
    for wnd in POOL_WINDOWS:
        while k < wnd:
            run = run + sp_ref[POOL_HIST - k]
            k += 1
        sums.append(run)
    win = _pool_window_of_lane(u.shape)
    ssel = sums[-1]
    for g in reversed(range(len(POOL_WINDOWS) - 1)):
        ssel = jnp.where(win == POOL_WINDOWS[g], sums[g], ssel)
    cnt = jnp.minimum(win, start_pos + 1).astype(F32)
    d = (ssel / cnt - u).astype(BF16)
    pool_o[...] = (jnp.dot(d, poolw_ref[...], preferred_element_type=F32) * pscale_ref[...]).astype(BF16)

    y = h_ref[...] * dww_ref[CONV_HIST:CONV_TAPS, :] + dwb_ref[...]
    for tap in range(CONV_HIST):
        y = y + sc_ref[tap] * dww_ref[tap:tap + 1, :]
    conv_o[...] = _conv_tail(y, lng_ref[...], lnb_ref[...], pww_ref[...], pwb_ref[...]).astype(BF16)


def _mix_decode(u, state_pool_tm, h, state_conv_tm, poolw_bd, pscale, dww, dwb, lng, lnb, pww, pwb, *, start_pos):
    db, c = u.shape
    args = (u, state_pool_tm, h, state_conv_tm, poolw_bd, pscale, dww, dwb, lng, lnb, pww, pwb)
    return pl.pallas_call(
        functools.partial(_mix_decode_kernel, start_pos=start_pos),
        grid=(1,),
        in_specs=[_const_spec(a.shape) for a in args],
        out_specs=[_const_spec((db, c))] * 2,
        out_shape=[jax.ShapeDtypeStruct((db, c), BF16)] * 2,
        compiler_params=_params("arbitrary"),
        name="mix_decode",
    )(*args)


def _merge_ffn_kernel(x_ref, att_ref, pool_ref, conv_ref, wo_ref, g_ref, wup_ref, wdn_ref, o_ref, *, ff_chunk):
    ca, cp = att_ref.shape[1], pool_ref.shape[1]
    m = jnp.dot(att_ref[...], wo_ref[0:ca, :], preferred_element_type=F32)
    m = m + jnp.dot(pool_ref[...], wo_ref[ca:ca + cp, :], preferred_element_type=F32)
    m = m + jnp.dot(conv_ref[...], wo_ref[ca + cp:, :], preferred_element_type=F32)
    x1 = x_ref[...] + m
    ms = jnp.mean(x1 * x1, axis=-1, keepdims=True)
    xn = (x1 * lax.rsqrt(ms + EPS) * g_ref[...]).astype(BF16)
    out = x1
    d_ff = wup_ref.shape[1]
    for c0 in range(0, d_ff, ff_chunk):
        hmid = jnp.dot(xn, wup_ref[:, c0:c0 + ff_chunk], preferred_element_type=F32)
        hmid = jnp.square(jnp.maximum(hmid, 0.0)).astype(BF16)
        out = out + jnp.dot(hmid, wdn_ref[c0:c0 + ff_chunk, :], preferred_element_type=F32)
    o_ref[...] = out


def _merge_ffn(x, att, pool, conv, wo_bf, g, wup_bf, wdn_bf, *, tm, ff_chunk):
    m, d = x.shape
    row = lambda c: pl.BlockSpec((tm, c), lambda i: (i, 0))
    single = lambda shape: pl.BlockSpec(shape, lambda i: (0, 0), pipeline_mode=pl.Buffered(1))
    return pl.pallas_call(
        functools.partial(_merge_ffn_kernel, ff_chunk=ff_chunk),
        grid=(m // tm,),
        in_specs=[row(d), row(att.shape[1]), row(pool.shape[1]), row(conv.shape[1]),
                  single(wo_bf.shape), single((1, d)), single(wup_bf.shape), single(wdn_bf.shape)],
        out_specs=row(d),
        out_shape=jax.ShapeDtypeStruct((m, d), F32),
        compiler_params=_params("parallel"),
        name="merge_ffn",
    )(x, att, pool, conv, wo_bf, g, wup_bf, wdn_bf)


def _block_diag(blocks):
    g, a, b = blocks.shape
    eye = jnp.eye(g, dtype=blocks.dtype)
    return (eye[:, None, :, None] * blocks[:, :, None, :]).reshape(g * a, g * b)


def _reverse_tri(n):
    return (jnp.arange(n)[None, :] >= jnp.arange(n)[:, None]).astype(BF16)


ATTN_TQ = 256
ATTN_HEADS_PER_STEP = 4
PROMPT_TM = 256
MIX_TT = 512
FF_CHUNK = 1024
DECODE_PAGES = 16


def kernel(x_prompt, x_sample, cache_k, cache_v, state_pool, state_conv, page_table, norm_mix_g, w_in, q_norm_g, k_norm_g, sb_bias, pool_w, pool_scale, conv_dw_w, conv_dw_b, conv_ln_g, conv_ln_b, conv_pw_w, conv_pw_b, w_out, norm_ffn_g, w_ffn_up, w_ffn_down):
    b, t, d = x_prompt.shape
    db, ts, _ = x_sample.shape
    assert ts == 1
    depth = w_in.shape[0]
    n_heads = sb_bias.shape[1]
    attn_ch = n_heads * HEAD_DIM
    pool_ch = pool_scale.shape[1]
    conv_ch = conv_dw_b.shape[1]
    past_len = page_table.shape[1] * PAGE_SIZE
    nk = t // ATTN_TQ

    xp = x_prompt.reshape(b * t, d)
    xs = x_sample.reshape(db, d)

    head_mean = _block_diag(jnp.full((n_heads, HEAD_DIM, HEAD_DIM), 1.0 / HEAD_DIM, BF16))
    tri_prompt = _reverse_tri(ATTN_TQ)
    tri_page = _reverse_tri(PAGE_SIZE).T
    cache_kt = cache_k.transpose(0, 1, 3, 4, 2)
    cache_vt = cache_v.transpose(0, 1, 3, 4, 2)
    q_pad = jnp.zeros((b, n_heads, HEAD_DIM, t), BF16).at[:, :, 0:2, :].set(1.0)
    row2 = lambda v: v.reshape(1, -1)

    outs = {n: [] for n in ("kp", "vp", "ks", "vs", "pp", "ps", "cp", "cs")}
    for l in range(depth):
        w_in_bf = w_in[l].astype(BF16)
        qg_t = row2(jnp.tile(q_norm_g[l], n_heads))
        kg_t = row2(jnp.tile(k_norm_g[l], n_heads))
        poolw_bd = _block_diag(pool_w[l]).astype(BF16)
        pww_bf = conv_pw_w[l].astype(BF16)
        mix_w = (poolw_bd, row2(pool_scale[l]), conv_dw_w[l], row2(conv_dw_b[l]),
                 row2(conv_ln_g[l]), row2(conv_ln_b[l]), pww_bf, row2(conv_pw_b[l]))
        inproj = functools.partial(_inproj, g=row2(norm_mix_g[l]), w_bf=w_in_bf, qg_t=qg_t, kg_t=kg_t,
                                   head_mean=head_mean, attn_ch=attn_ch, pool_ch=pool_ch, conv_ch=conv_ch)
        ffn_w = (w_out[l].astype(BF16), row2(norm_ffn_g[l]), w_ffn_up[l].astype(BF16), w_ffn_down[l].astype(BF16))
        bias2 = sb_bias[l] * LOG2E
        bias_hi = bias2.astype(BF16)
        bias_lo = (bias2 - bias_hi.astype(F32)).astype(BF16)

        qp, kpf, kpb, vpf, vpb, up, hp = inproj(xp, tm=PROMPT_TM)
        qs, ksf, _, vsf, _, us, hs = inproj(xs, tm=db)

        qt = jnp.concatenate([qp.reshape(b, t, n_heads, HEAD_DIM).transpose(0, 2, 3, 1), q_pad], axis=2)
        k_pad = jnp.zeros((n_heads, HEAD_DIM), BF16).at[:, 0].set(bias_hi).at[:, 1].set(bias_lo)
        k_hm = kpb.reshape(b, nk, ATTN_TQ, n_heads, HEAD_DIM).transpose(0, 3, 1, 2, 4)
        k_hm = jnp.concatenate(
            [k_hm, jnp.broadcast_to(k_pad[None, :, None, None, :], k_hm.shape)], axis=-1)
        vt_hm = vpb.reshape(b, nk, ATTN_TQ, n_heads, HEAD_DIM).transpose(0, 3, 1, 4, 2)
        att_t = _attn_prompt(qt, k_hm, vt_hm, tri_prompt, tq=ATTN_TQ, heads=ATTN_HEADS_PER_STEP)
        att_p = att_t.transpose(0, 3, 1, 2).reshape(b * t, attn_ch).astype(BF16)

        q_rep = jnp.broadcast_to(qs.astype(F32).reshape(db, n_heads, HEAD_DIM, 1), (db, n_heads, HEAD_DIM, PAGE_SIZE))
        bias_rep = jnp.broadcast_to(bias2[:, None], (n_heads, PAGE_SIZE))
        att_s = _attn_decode(page_table, q_rep, bias_rep, tri_page, cache_kt, cache_vt,
                             layer=l, pages=DECODE_PAGES).reshape(db, attn_ch).astype(BF16)

        pool_p, conv_p = _mix_prompt(up.reshape(b, t, pool_ch), hp.reshape(b, t, conv_ch), *mix_w, tt=MIX_TT)
        pool_s, conv_s = _mix_decode(us, state_pool[l].transpose(1, 0, 2), hs, state_conv[l].transpose(1, 0, 2),
                                     *mix_w, start_pos=past_len)

        xp = _merge_ffn(xp, att_p, pool_p.reshape(b * t, pool_ch), conv_p.reshape(b * t, conv_ch), *ffn_w,
                        tm=PROMPT_TM, ff_chunk=FF_CHUNK)
        xs = _merge_ffn(xs, att_s, pool_s, conv_s, *ffn_w, tm=db, ff_chunk=FF_CHUNK)

        outs["kp"].append(kpf.reshape(b, t, n_heads, HEAD_DIM))
        outs["vp"].append(vpf.reshape(b, t, n_heads, HEAD_DIM))
        outs["ks"].append(ksf.reshape(db, ts, n_heads, HEAD_DIM))
        outs["vs"].append(vsf.reshape(db, ts, n_heads, HEAD_DIM))
        outs["pp"].append(up.reshape(b, t, pool_ch)[:, t - POOL_HIST:])
        outs["ps"].append(jnp.concatenate([state_pool[l][:, 1:], us[:, None, :]], axis=1))
        outs["cp"].append(hp.reshape(b, t, conv_ch)[:, t - CONV_HIST:])
        outs["cs"].append(jnp.concatenate([state_conv[l][:, 1:], hs[:, None, :]], axis=1))

    return (xp.reshape(b, t, d), xs.reshape(db, ts, d),
            jnp.stack(outs["kp"]), jnp.stack(outs["vp"]), jnp.stack(outs["ks"]), jnp.stack(outs["vs"]),
            jnp.stack(outs["pp"]), jnp.stack(outs["ps"]), jnp.stack(outs["cp"]), jnp.stack(outs["cs"]))
```

```python
import functools

import jax
import jax.numpy as jnp
from jax import lax
from jax.experimental import pallas as pl
from jax.experimental.pallas import tpu as pltpu

F32 = jnp.float32
BF16 = jnp.bfloat16

EPS = 1e-6
LOG2E = 1.4426950408889634
HEAD_DIM = 64
POOL_WINDOWS = (2, 4, 8, 16)
POOL_HIST = max(POOL_WINDOWS) - 1
CONV_TAPS = 31
CONV_HIST = CONV_TAPS - 1
PAGE_SIZE = 128

V7X_VMEM_LIMIT_BYTES = 56 * 1024 * 1024
LANES = 128


def _params(*sem):
    return pltpu.CompilerParams(dimension_semantics=sem, vmem_limit_bytes=V7X_VMEM_LIMIT_BYTES)


def _const_spec(shape):
    nd = len(shape)
    return pl.BlockSpec(shape, lambda *_: (0,) * nd)


def _inproj_math(x, g, w, qg, kg, hm, *, attn_ch, pool_ch, conv_ch):
    ms = jnp.mean(x * x, axis=-1, keepdims=True)
    xn = (x * lax.rsqrt(ms + EPS) * g).astype(BF16)
    z = jnp.dot(xn, w, preferred_element_type=F32)
    c0, c1, c2 = attn_ch, 2 * attn_ch, 3 * attn_ch
    c3 = c2 + pool_ch
    c4 = c3 + conv_ch
    q, k, v = z[:, :c0], z[:, c0:c1], z[:, c1:c2]
    u, a, gate = z[:, c2:c3], z[:, c3:c4], z[:, c4:]

    def head_norm(t, gain):
        msq = jnp.dot((t * t).astype(BF16), hm, preferred_element_type=F32)
        return t * lax.rsqrt(msq + EPS) * gain

    q_scaled = head_norm(q, qg) * (HEAD_DIM ** -0.5 * LOG2E)
    return q_scaled, head_norm(k, kg), v, u, a * jax.nn.sigmoid(gate)


def _inproj_decode_kernel(x_ref, g_ref, w_ref, qg_ref, kg_ref, hm_ref,
                          q_ref, k_ref, v_ref, u_ref, h_ref, **dims):
    q, k, v, u, h = _inproj_math(x_ref[...], g_ref[...], w_ref[...], qg_ref[...], kg_ref[...], hm_ref[...], **dims)
    q_ref[...] = q
    k_ref[...] = k
    v_ref[...] = v
    u_ref[...] = u
    h_ref[...] = h


def _inproj_decode(x, g, w_bf, qg_t, kg_t, head_mean, *, attn_ch, pool_ch, conv_ch):
    m, d = x.shape
    widths = (attn_ch, attn_ch, attn_ch, pool_ch, conv_ch)
    args = (x, g, w_bf, qg_t, kg_t, head_mean)
    return pl.pallas_call(
        functools.partial(_inproj_decode_kernel, attn_ch=attn_ch, pool_ch=pool_ch, conv_ch=conv_ch),
        grid=(1,),
        in_specs=[_const_spec(a.shape) for a in args],
        out_specs=[_const_spec((m, c)) for c in widths],
        out_shape=[jax.ShapeDtypeStruct((m, c), F32) for c in widths],
        compiler_params=_params("arbitrary"),
        name="inproj_decode",
    )(*args)


def _inproj_prompt_kernel(x_ref, g_ref, w_ref, qg_ref, kg_ref, hm_ref, place_ref, kpad_ref, qpad_ref,
                          qt_ref, ka_ref, vt_ref, kf_ref, vf_ref, u_ref, h_ref, **dims):
    q, k, v, u, h = _inproj_math(x_ref[...], g_ref[...], w_ref[...], qg_ref[...], kg_ref[...], hm_ref[...], **dims)
    u_ref[...] = u
    h_ref[...] = h
    ka = (jnp.dot(k.astype(BF16), place_ref[...], preferred_element_type=F32) + kpad_ref[...]).astype(BF16)
    qt, kt, vt = q.T, k.T, v.T
    for hd in range(qt_ref.shape[0]):
        rows = slice(hd * HEAD_DIM, (hd + 1) * HEAD_DIM)
        qt_ref[hd, 0:HEAD_DIM, :] = qt[rows].astype(BF16)
        qt_ref[hd, HEAD_DIM:, :] = qpad_ref[...]
        ka_ref[hd] = ka[:, hd * LANES:(hd + 1) * LANES]
        vt_ref[hd] = vt[rows].astype(BF16)
        kf_ref[hd] = kt[rows]
        vf_ref[hd] = vt[rows]


def _inproj_prompt(x, g, w_bf, qg_t, kg_t, head_mean, place, kpad, qpad, *, tm, attn_ch, pool_ch, conv_ch):
    b, t, d = x.shape
    nt = t // tm
    nh = attn_ch // HEAD_DIM
    tok_minor = pl.BlockSpec((None, nh, HEAD_DIM, tm), lambda i: (i // nt, 0, 0, i % nt))
    row = lambda c: pl.BlockSpec((None, tm, c), lambda i: (i // nt, i % nt, 0))
    consts = (g, w_bf, qg_t, kg_t, head_mean, place, kpad, qpad)
    return pl.pallas_call(
        functools.partial(_inproj_prompt_kernel, attn_ch=attn_ch, pool_ch=pool_ch, conv_ch=conv_ch),
        grid=(b * nt,),
        in_specs=[row(d)] + [_const_spec(a.shape) for a in consts],
        out_specs=[
            pl.BlockSpec((None, nh, LANES, tm), lambda i: (i // nt, 0, 0, i % nt)),
            pl.BlockSpec((None, nh, None, tm, LANES), lambda i: (i // nt, 0, i % nt, 0, 0)),
            pl.BlockSpec((None, nh, None, HEAD_DIM, tm), lambda i: (i // nt, 0, i % nt, 0, 0)),
            tok_minor, tok_minor, row(pool_ch), row(conv_ch)],
        out_shape=[
            jax.ShapeDtypeStruct((b, nh, LANES, t), BF16),
            jax.ShapeDtypeStruct((b, nh, nt, tm, LANES), BF16),
            jax.ShapeDtypeStruct((b, nh, nt, HEAD_DIM, tm), BF16),
            jax.ShapeDtypeStruct((b, nh, HEAD_DIM, t), F32),
            jax.ShapeDtypeStruct((b, nh, HEAD_DIM, t), F32),
            jax.ShapeDtypeStruct((b, t, pool_ch), F32),
            jax.ShapeDtypeStruct((b, t, conv_ch), F32),
        ],
        compiler_params=_params("parallel"),
        name="inproj_prompt",
    )(x, *consts)


SOFTPLUS2_CLAMP = 100.0


def _softplus2(z2):
    return jnp.maximum(z2, jnp.log(1.0 + jnp.exp2(jnp.minimum(z2, SOFTPLUS2_CLAMP))) * LOG2E)


def _attn_prompt_kernel(qt_ref, k_ref, vt_ref, tri_ref, o_ref, r_ref, za_ref, zb_ref):
    i = pl.program_id(2)
    heads = qt_ref.shape[0]
    tri = tri_ref[...]
    tk, tq = tri.shape[0], qt_ref.shape[-1]

    def scores(j, z_ref):
        for g in range(heads):
            z_ref[g] = jnp.dot(k_ref[g, j], qt_ref[g], preferred_element_type=F32)

    def step(j, zc_ref, zn_ref, diagonal=False):
        scores(jnp.maximum(j - 1, 0), zn_ref)
        if diagonal:
            vis = lax.broadcasted_iota(jnp.int32, (tk, tq), 0) < lax.broadcasted_iota(jnp.int32, (tk, tq), 1)
            sps = [jnp.where(vis, _softplus2(zc_ref[g]), 0.0) for g in range(heads)]
        else:
            sps = [_softplus2(zc_ref[g]) for g in range(heads)]
        ss = [jnp.dot(tri, sp.astype(BF16), preferred_element_type=F32) for sp in sps]
        for g in range(heads):
            w = jnp.exp2(zc_ref[g] - ss[g])
            if diagonal:
                w = jnp.where(vis, w, 0.0)
            pv = jnp.dot(vt_ref[g, j], w.astype(BF16), preferred_element_type=F32)
            if diagonal:
                o_ref[g] = pv
                r_ref[g] = ss[g][0:1, :]
            else:
                r = r_ref[g]
                o_ref[g] += pv * jnp.exp2(-r)
                r_ref[g] = r + ss[g][0:1, :]

    scores(i, za_ref)
    step(i, za_ref, zb_ref, diagonal=True)

    def body(p, carry):
        j = i - 1 - 2 * p
        step(j, zb_ref, za_ref)
        step(j - 1, za_ref, zb_ref)
        return carry

    lax.fori_loop(0, i // 2, body, 0)

    @pl.when(i % 2 == 1)
    def _():
        step(0, zb_ref, za_ref)


def _attn_prompt(qt, k_hm, vt_hm, tri, *, tq, heads):
    b, h, da, t = qt.shape
    nk, tk = k_hm.shape[2], k_hm.shape[3]
    dh = vt_hm.shape[3]
    return pl.pallas_call(
        _attn_prompt_kernel,
        grid=(b, h // heads, t // tq),
        in_specs=[
            pl.BlockSpec((None, heads, da, tq), lambda bi, hi, i: (bi, hi, 0, i)),
            pl.BlockSpec((None, heads, nk, tk, da), lambda bi, hi, i: (bi, hi, 0, 0, 0)),
            pl.BlockSpec((None, heads, nk, dh, tk), lambda bi, hi, i: (bi, hi, 0, 0, 0)),
            _const_spec((tk, tk)),
        ],
        out_specs=pl.BlockSpec((None, heads, dh, tq), lambda bi, hi, i: (bi, hi, 0, i)),
        out_shape=jax.ShapeDtypeStruct((b, h, dh, t), F32),
        scratch_shapes=[pltpu.VMEM((heads, 1, tq), F32)] + [pltpu.VMEM((heads, tk, tq), F32)] * 2,
        compiler_params=_params("parallel", "parallel", "arbitrary"),
        name="attn_prompt",
    )(qt, k_hm, vt_hm, tri)


def _attn_decode_kernel(pt_ref, q_ref, bias_ref, tri_ref, *refs, pages):
    del pt_ref
    k_refs, v_refs = refs[:pages], refs[pages:2 * pages]
    o_ref, acc_ref, r_ref = refs[2 * pages:]
    j = pl.program_id(1)

    @pl.when(j == 0)
    def _():
        acc_ref[...] = jnp.zeros_like(acc_ref)
        r_ref[...] = jnp.zeros_like(r_ref)

    q = q_ref[...]
    bias = bias_ref[...]
    tri = tri_ref[...]
    r = r_ref[...]
    for p in reversed(range(pages)):
        z = jnp.sum(k_refs[p][...] * q, axis=1) + bias
        s = jnp.dot(_softplus2(z).astype(BF16), tri, preferred_element_type=F32)
        w = jnp.exp2(z - s - r)
        acc_ref[...] += w[:, None, :] * v_refs[p][...]
        r = r + s[:, 0:1]
    r_ref[...] = r

    @pl.when(j == pl.num_programs(1) - 1)
    def _():
        o_ref[...] = jnp.sum(acc_ref[...], axis=-1)


def _attn_decode(page_table, q_rep, bias_rep, tri, cache_kt, cache_vt, *, layer, pages):
    db, n_pages = page_table.shape
    nh, dh, page = cache_kt.shape[2:]
    groups = n_pages // pages

    def page_spec(p):
        return pl.BlockSpec(
            (None, None, nh, dh, page),
            lambda b, j, pt: (layer, pt[b, (groups - 1 - j) * pages + p], 0, 0, 0))

    grid_spec = pltpu.PrefetchScalarGridSpec(
        num_scalar_prefetch=1,
        grid=(db, groups),
        in_specs=[
            pl.BlockSpec((None, nh, dh, page), lambda b, j, pt: (b, 0, 0, 0)),
            pl.BlockSpec((nh, page), lambda b, j, pt: (0, 0)),
            pl.BlockSpec((page, page), lambda b, j, pt: (0, 0)),
        ] + [page_spec(p) for p in range(pages)] * 2,
        out_specs=pl.BlockSpec((None, nh, dh), lambda b, j, pt: (b, 0, 0)),
        scratch_shapes=[pltpu.VMEM((nh, dh, page), F32), pltpu.VMEM((nh, 1), F32)],
    )
    return pl.pallas_call(
        functools.partial(_attn_decode_kernel, pages=pages),
        grid_spec=grid_spec,
        out_shape=jax.ShapeDtypeStruct((db, nh, dh), F32),
        compiler_params=_params("parallel", "arbitrary"),
        name="attn_decode",
    )(page_table, q_rep, bias_rep, tri, *([cache_kt] * pages), *([cache_vt] * pages))


def _conv_tail(y, lng, lnb, pww, pwb):
    mu = jnp.mean(y, axis=-1, keepdims=True)
    yc = y - mu
    var = jnp.mean(yc * yc, axis=-1, keepdims=True)
    yn = yc * lax.rsqrt(var + EPS) * lng + lnb
    yn = (yn * jax.nn.sigmoid(yn)).astype(BF16)
    return jnp.dot(yn, pww, preferred_element_type=F32) + pwb


def _pool_window_of_lane(shape):
    lane = lax.broadcasted_iota(jnp.int32, shape, len(shape) - 1)
    gw = shape[-1] // len(POOL_WINDOWS)
    win = jnp.full(shape, POOL_WINDOWS[-1], jnp.int32)
    for g in reversed(range(len(POOL_WINDOWS) - 1)):
        win = jnp.where(lane < (g + 1) * gw, POOL_WINDOWS[g], win)
    return win


HALO = 32


def _mix_prompt_kernel(u_ref, up_ref, h_ref, hp_ref, poolw_ref, pscale_ref, dww_ref, dwb_ref,
                       lng_ref, lnb_ref, pww_ref, pwb_ref, pool_o, conv_o, uext, hext):
    i = pl.program_id(1)
    tt, c = u_ref.shape

    @pl.when(i == 0)
    def _():
        uext[0:HALO, :] = jnp.zeros((HALO, c), F32)
        hext[0:HALO, :] = jnp.zeros((HALO, c), F32)

    @pl.when(i > 0)
    def _():
        uext[0:HALO, :] = up_ref[...]
        hext[0:HALO, :] = hp_ref[...]

    u = u_ref[...]
    uext[HALO:, :] = u
    hext[HALO:, :] = h_ref[...]

    sums = []
    run = u
    k = 1
    for wnd in POOL_WINDOWS:
        while k < wnd:
            run = run + uext[pl.ds(HALO - k, tt), :]
            k += 1
        sums.append(run)
    win = _pool_window_of_lane((tt, c))
    ssel = sums[-1]
    for g in reversed(range(len(POOL_WINDOWS) - 1)):
        ssel = jnp.where(win == POOL_WINDOWS[g], sums[g], ssel)
    pos = i * tt + lax.broadcasted_iota(jnp.int32, (tt, c), 0)
    cnt = jnp.minimum(win, pos + 1).astype(F32)
    d = (ssel / cnt - u).astype(BF16)
    pool_o[...] = (jnp.dot(d, poolw_ref[...], preferred_element_type=F32) * pscale_ref[...]).astype(BF16)

    y = jnp.zeros((tt, c), F32) + dwb_ref[...]
    for tap in range(CONV_TAPS):
        y = y + hext[pl.ds(HALO - CONV_HIST + tap, tt), :] * dww_ref[tap:tap + 1, :]
    conv_o[...] = _conv_tail(y, lng_ref[...], lnb_ref[...], pww_ref[...], pwb_ref[...]).astype(BF16)


def _mix_prompt(u, h, poolw_bd, pscale, dww, dwb, lng, lnb, pww, pwb, *, tt):
    b, t, c = u.shape
    r = tt // HALO
    cur = pl.BlockSpec((None, tt, c), lambda bi, i: (bi, i, 0))
    prev = pl.BlockSpec((None, HALO, c), lambda bi, i: (bi, jnp.maximum(i * r - 1, 0), 0))
    vec = _const_spec((1, c))
    return pl.pallas_call(
        _mix_prompt_kernel,
        grid=(b, t // tt),
        in_specs=[cur, prev, cur, prev, _const_spec((c, c)), vec, _const_spec(dww.shape), vec,
                  vec, vec, _const_spec((c, c)), vec],
        out_specs=[cur, cur],
        out_shape=[jax.ShapeDtypeStruct((b, t, c), BF16)] * 2,
        scratch_shapes=[pltpu.VMEM((HALO + tt, c), F32)] * 2,
        compiler_params=_params("parallel", "arbitrary"),
        name="mix_prompt",
    )(u, u, h, h, poolw_bd, pscale, dww, dwb, lng, lnb, pww, pwb)


def _mix_decode_kernel(u_ref, sp_ref, h_ref, sc_ref, poolw_ref, pscale_ref, dww_ref, dwb_ref,
                       lng_ref, lnb_ref, pww_ref, pwb_ref, pool_o, conv_o, *, start_pos):
    u = u_ref[...]
    sums = []
    run = u
    k = 1
    for wnd in POOL_WINDOWS:
        while k < wnd:
            run = run + sp_ref[POOL_HIST - k]
            k += 1
        sums.append(run)
    win = _pool_window_of_lane(u.shape)
    ssel = sums[-1]
    for g in reversed(range(len(POOL_WINDOWS) - 1)):
        ssel = jnp.where(win == POOL_WINDOWS[g], sums[g], ssel)
    cnt = jnp.minimum(win, start_pos + 1).astype(F32)
    d = (ssel / cnt - u).astype(BF16)
    pool_o[...] = (jnp.dot(d, poolw_ref[...], preferred_element_type=F32) * pscale_ref[...]).astype(BF16)

    y = h_ref[...] * dww_ref[CONV_HIST:CONV_TAPS, :] + dwb_ref[...]
    for tap in range(CONV_HIST):
        y = y + sc_ref[tap] * dww_ref[tap:tap + 1, :]
    conv_o[...] = _conv_tail(y, lng_ref[...], lnb_ref[...], pww_ref[...], pwb_ref[...]).astype(BF16)


def _mix_decode(u, state_pool_tm, h, state_conv_tm, poolw_bd, pscale, dww, dwb, lng, lnb, pww, pwb, *, start_pos):
    db, c = u.shape
    args = (u, state_pool_tm, h, state_conv_tm, poolw_bd, pscale, dww, dwb, lng, lnb, pww, pwb)
    return pl.pallas_call(
        functools.partial(_mix_decode_kernel, start_pos=start_pos),
        grid=(1,),
        in_specs=[_const_spec(a.shape) for a in args],
        out_specs=[_const_spec((db, c))] * 2,
        out_shape=[jax.ShapeDtypeStruct((db, c), BF16)] * 2,
        compiler_params=_params("arbitrary"),
        name="mix_decode",
    )(*args)


def _merge_ffn_kernel(x_ref, att_ref, pool_ref, conv_ref, wo_ref, g_ref, wup_ref, wdn_ref, o_ref, *, ff_chunk):
    cp = pool_ref.shape[1]
    if len(att_ref.shape) == 3:
        nh, dh, rows = att_ref.shape
        att = att_ref[...].reshape(nh * dh, rows).T.astype(BF16)
    else:
        att = att_ref[...]
    ca = att.shape[1]
    m = jnp.dot(att, wo_ref[0:ca, :], preferred_element_type=F32)
    m = m + jnp.dot(pool_ref[...], wo_ref[ca:ca + cp, :], preferred_element_type=F32)
    m = m + jnp.dot(conv_ref[...], wo_ref[ca + cp:, :], preferred_element_type=F32)
    x1 = x_ref[...] + m
    ms = jnp.mean(x1 * x1, axis=-1, keepdims=True)
    xn = (x1 * lax.rsqrt(ms + EPS) * g_ref[...]).astype(BF16)
    out = x1
    d_ff = wup_ref.shape[1]
    for c0 in range(0, d_ff, ff_chunk):
        hmid = jnp.dot(xn, wup_ref[:, c0:c0 + ff_chunk], preferred_element_type=F32)
        hmid = jnp.square(jnp.maximum(hmid, 0.0)).astype(BF16)
        out = out + jnp.dot(hmid, wdn_ref[c0:c0 + ff_chunk, :], preferred_element_type=F32)
    o_ref[...] = out


def _merge_ffn(x, att, pool, conv, wo_bf, g, wup_bf, wdn_bf, *, tm, ff_chunk):
    m, d = x.shape
    row = lambda c: pl.BlockSpec((tm, c), lambda i: (i, 0))
    single = lambda shape: pl.BlockSpec(shape, lambda i: (0, 0), pipeline_mode=pl.Buffered(1))
    if att.ndim == 4:
        nt = att.shape[3] // tm
        att_spec = pl.BlockSpec((None,) + att.shape[1:3] + (tm,), lambda i: (i // nt, 0, 0, i % nt))
    else:
        att_spec = row(att.shape[1])
    return pl.pallas_call(
        functools.partial(_merge_ffn_kernel, ff_chunk=ff_chunk),
        grid=(m // tm,),
        in_specs=[row(d), att_spec, row(pool.shape[1]), row(conv.shape[1]),
                  single(wo_bf.shape), single((1, d)), single(wup_bf.shape), single(wdn_bf.shape)],
        out_specs=row(d),
        out_shape=jax.ShapeDtypeStruct((m, d), F32),
        compiler_params=_params("parallel"),
        name="merge_ffn",
    )(x, att, pool, conv, wo_bf, g, wup_bf, wdn_bf)


def _block_diag(blocks):
    g, a, b = blocks.shape
    eye = jnp.eye(g, dtype=blocks.dtype)
    return (eye[:, None, :, None] * blocks[:, :, None, :]).reshape(g * a, g * b)


def _reverse_tri(n):
    return (jnp.arange(n)[None, :] >= jnp.arange(n)[:, None]).astype(BF16)


ATTN_TQ = 256
ATTN_HEADS_PER_STEP = 4
PROMPT_TM = 256
MIX_TT = 512
FF_CHUNK = 1024
DECODE_PAGES = 16


def kernel(x_prompt, x_sample, cache_k, cache_v, state_pool, state_conv, page_table, norm_mix_g, w_in, q_norm_g, k_norm_g, sb_bias, pool_w, pool_scale, conv_dw_w, conv_dw_b, conv_ln_g, conv_ln_b, conv_pw_w, conv_pw_b, w_out, norm_ffn_g, w_ffn_up, w_ffn_down):
    b, t, d = x_prompt.shape
    db, ts, _ = x_sample.shape
    assert ts == 1
    assert PROMPT_TM == ATTN_TQ
    depth = w_in.shape[0]
    n_heads = sb_bias.shape[1]
    attn_ch = n_heads * HEAD_DIM
    pool_ch = pool_scale.shape[1]
    conv_ch = conv_dw_b.shape[1]
    past_len = page_table.shape[1] * PAGE_SIZE

    xp = x_prompt.reshape(b * t, d)
    xs = x_sample.reshape(db, d)

    head_mean = _block_diag(jnp.full((n_heads, HEAD_DIM, HEAD_DIM), 1.0 / HEAD_DIM, BF16))
    tri_prompt = _reverse_tri(ATTN_TQ)
    tri_page = _reverse_tri(PAGE_SIZE).T
    cache_kt = cache_k.transpose(0, 1, 3, 4, 2)
    cache_vt = cache_v.transpose(0, 1, 3, 4, 2)
    ch = jnp.arange(attn_ch)
    place = (jnp.arange(n_heads * LANES)[None, :] == ((ch // HEAD_DIM) * LANES + ch % HEAD_DIM)[:, None]).astype(BF16)
    qpad = jnp.zeros((LANES - HEAD_DIM, PROMPT_TM), BF16).at[0:2, :].set(1.0)
    row2 = lambda v: v.reshape(1, -1)

    outs = {n: [] for n in ("kp", "vp", "ks", "vs", "pp", "ps", "cp", "cs")}
    for l in range(depth):
        poolw_bd = _block_diag(pool_w[l]).astype(BF16)
        mix_w = (poolw_bd, row2(pool_scale[l]), conv_dw_w[l], row2(conv_dw_b[l]),
                 row2(conv_ln_g[l]), row2(conv_ln_b[l]), conv_pw_w[l].astype(BF16), row2(conv_pw_b[l]))
        in_w = (row2(norm_mix_g[l]), w_in[l].astype(BF16), row2(jnp.tile(q_norm_g[l], n_heads)),
                row2(jnp.tile(k_norm_g[l], n_heads)), head_mean)
        dims = dict(attn_ch=attn_ch, pool_ch=pool_ch, conv_ch=conv_ch)
        ffn_w = (w_out[l].astype(BF16), row2(norm_ffn_g[l]), w_ffn_up[l].astype(BF16), w_ffn_down[l].astype(BF16))
        bias2 = sb_bias[l] * LOG2E
        bias_hi = bias2.astype(BF16)
        bias_lo = (bias2 - bias_hi.astype(F32)).astype(BF16)
        kpad = jnp.zeros((n_heads, LANES), F32).at[:, HEAD_DIM].set(bias_hi.astype(F32))
        kpad = kpad.at[:, HEAD_DIM + 1].set(bias_lo.astype(F32)).reshape(1, n_heads * LANES)

        qt, k_aug, vt, kpt, vpt, up, hp = _inproj_prompt(x_prompt if l == 0 else xp.reshape(b, t, d), *in_w,
                                                         place, kpad, qpad, tm=PROMPT_TM, **dims)
        qs, ksf, vsf, us, hs = _inproj_decode(xs, *in_w, **dims)

        att_t = _attn_prompt(qt, k_aug, vt, tri_prompt, tq=ATTN_TQ, heads=ATTN_HEADS_PER_STEP)

        q_rep = jnp.broadcast_to(qs.reshape(db, n_heads, HEAD_DIM, 1), (db, n_heads, HEAD_DIM, PAGE_SIZE))
        bias_rep = jnp.broadcast_to(bias2[:, None], (n_heads, PAGE_SIZE))
        att_s = _attn_decode(page_table, q_rep, bias_rep, tri_page, cache_kt, cache_vt,
                             layer=l, pages=DECODE_PAGES).reshape(db, attn_ch).astype(BF16)

        pool_p, conv_p = _mix_prompt(up, hp, *mix_w, tt=MIX_TT)
        pool_s, conv_s = _mix_decode(us, state_pool[l].transpose(1, 0, 2), hs, state_conv[l].transpose(1, 0, 2),
                                     *mix_w, start_pos=past_len)

        xp = _merge_ffn(xp, att_t, pool_p.reshape(b * t, pool_ch), conv_p.reshape(b * t, conv_ch), *ffn_w,
                        tm=PROMPT_TM, ff_chunk=FF_CHUNK)
        xs = _merge_ffn(xs, att_s, pool_s, conv_s, *ffn_w, tm=db, ff_chunk=FF_CHUNK)

        outs["kp"].append(kpt.transpose(0, 3, 1, 2))
        outs["vp"].append(vpt.transpose(0, 3, 1, 2))
        outs["ks"].append(ksf.reshape(db, ts, n_heads, HEAD_DIM))
        outs["vs"].append(vsf.reshape(db, ts, n_heads, HEAD_DIM))
        outs["pp"].append(up[:, t - POOL_HIST:])
        outs["ps"].append(jnp.concatenate([state_pool[l][:, 1:], us[:, None, :]], axis=1))
        outs["cp"].append(hp[:, t - CONV_HIST:])
        outs["cs"].append(jnp.concatenate([state_conv[l][:, 1:], hs[:, None, :]], axis=1))

    return (xp.reshape(b, t, d), xs.reshape(db, ts, d),
            jnp.stack(outs["kp"]), jnp.stack(outs["vp"]), jnp.stack(outs["ks"]), jnp.stack(outs["vs"]),
            jnp.stack(outs["pp"]), jnp.stack(outs["ps"]), jnp.stack(outs["cp"]), jnp.stack(outs["cs"]))
```

```python
import functools

import jax
import jax.numpy as jnp
from jax import lax
from jax.experimental import pallas as pl
from jax.experimental.pallas import tpu as pltpu

F32 = jnp.float32
BF16 = jnp.bfloat16

EPS = 1e-6
LOG2E = 1.4426950408889634
HEAD_DIM = 64
POOL_WINDOWS = (2, 4, 8, 16)
POOL_HIST = max(POOL_WINDOWS) - 1
CONV_TAPS = 31
CONV_HIST = CONV_TAPS - 1
PAGE_SIZE = 128

V7X_VMEM_LIMIT_BYTES = 56 * 1024 * 1024
LANES = 128
SUBLANES = 8


def _params(*sem):
    return pltpu.CompilerParams(dimension_semantics=sem, vmem_limit_bytes=V7X_VMEM_LIMIT_BYTES)


def _const_spec(shape):
    nd = len(shape)
    return pl.BlockSpec(shape, lambda *_: (0,) * nd)


def _inproj_math(x, g, w, qg, kg, hm, *, attn_ch, pool_ch, conv_ch):
    ms = jnp.mean(x * x, axis=-1, keepdims=True)
    xn = (x * lax.rsqrt(ms + EPS) * g).astype(BF16)
    z = jnp.dot(xn, w, preferred_element_type=F32)
    c0, c1, c2 = attn_ch, 2 * attn_ch, 3 * attn_ch
    c3 = c2 + pool_ch
    c4 = c3 + conv_ch
    q, k, v = z[:, :c0], z[:, c0:c1], z[:, c1:c2]
    u, a, gate = z[:, c2:c3], z[:, c3:c4], z[:, c4:]

    def head_norm(t, gain):
        msq = jnp.dot((t * t).astype(BF16), hm, preferred_element_type=F32)
        return t * lax.rsqrt(msq + EPS) * gain

    q_scaled = head_norm(q, qg) * (HEAD_DIM ** -0.5 * LOG2E)
    return q_scaled, head_norm(k, kg), v, u, a * jax.nn.sigmoid(gate)


def _inproj_decode_kernel(x_ref, g_ref, w_ref, qg_ref, kg_ref, hm_ref,
                          q_ref, k_ref, v_ref, u_ref, h_ref, **dims):
    q, k, v, u, h = _inproj_math(x_ref[...], g_ref[...], w_ref[...], qg_ref[...], kg_ref[...], hm_ref[...], **dims)
    q_ref[...] = q
    k_ref[...] = k
    v_ref[...] = v
    u_ref[...] = u
    h_ref[...] = h


def _inproj_decode(x, g, w_bf, qg_t, kg_t, head_mean, *, attn_ch, pool_ch, conv_ch):
    m, d = x.shape
    widths = (attn_ch, attn_ch, attn_ch, pool_ch, conv_ch)
    args = (x, g, w_bf, qg_t, kg_t, head_mean)
    return pl.pallas_call(
        functools.partial(_inproj_decode_kernel, attn_ch=attn_ch, pool_ch=pool_ch, conv_ch=conv_ch),
        grid=(1,),
        in_specs=[_const_spec(a.shape) for a in args],
        out_specs=[_const_spec((m, c)) for c in widths],
        out_shape=[jax.ShapeDtypeStruct((m, c), F32) for c in widths],
        compiler_params=_params("arbitrary"),
        name="inproj_decode",
    )(*args)


def _inproj_prompt_kernel(x_ref, g_ref, w_ref, qg_ref, kg_ref, hm_ref, place_ref, kpad_ref, qpad_ref,
                          qt_ref, ka_ref, vt_ref, kf_ref, vf_ref, u_ref, h_ref, **dims):
    q, k, v, u, h = _inproj_math(x_ref[...], g_ref[...], w_ref[...], qg_ref[...], kg_ref[...], hm_ref[...], **dims)
    u_ref[...] = u
    h_ref[...] = h
    ka = (jnp.dot(k.astype(BF16), place_ref[...], preferred_element_type=F32) + kpad_ref[...]).astype(BF16)
    qt, kt, vt = q.T, k.T, v.T
    for hd in range(qt_ref.shape[0]):
        rows = slice(hd * HEAD_DIM, (hd + 1) * HEAD_DIM)
        qt_ref[hd, 0:HEAD_DIM, :] = qt[rows].astype(BF16)
        qt_ref[hd, HEAD_DIM:, :] = qpad_ref[...]
        ka_ref[hd] = ka[:, hd * LANES:(hd + 1) * LANES]
        vt_ref[hd] = vt[rows].astype(BF16)
        kf_ref[hd] = kt[rows]
        vf_ref[hd] = vt[rows]


def _inproj_prompt(x, g, w_bf, qg_t, kg_t, head_mean, place, kpad, qpad, *, tm, attn_ch, pool_ch, conv_ch):
    b, t, d = x.shape
    nt = t // tm
    nh = attn_ch // HEAD_DIM
    tok_minor = pl.BlockSpec((None, nh, HEAD_DIM, tm), lambda i: (i // nt, 0, 0, i % nt))
    row = lambda c: pl.BlockSpec((None, tm, c), lambda i: (i // nt, i % nt, 0))
    consts = (g, w_bf, qg_t, kg_t, head_mean, place, kpad, qpad)
    return pl.pallas_call(
        functools.partial(_inproj_prompt_kernel, attn_ch=attn_ch, pool_ch=pool_ch, conv_ch=conv_ch),
        grid=(b * nt,),
        in_specs=[row(d)] + [_const_spec(a.shape) for a in consts],
        out_specs=[
            pl.BlockSpec((None, nh, LANES, tm), lambda i: (i // nt, 0, 0, i % nt)),
            pl.BlockSpec((None, nh, None, tm, LANES), lambda i: (i // nt, 0, i % nt, 0, 0)),
            pl.BlockSpec((None, nh, None, HEAD_DIM, tm), lambda i: (i // nt, 0, i % nt, 0, 0)),
            tok_minor, tok_minor, row(pool_ch), row(conv_ch)],
        out_shape=[
            jax.ShapeDtypeStruct((b, nh, LANES, t), BF16),
            jax.ShapeDtypeStruct((b, nh, nt, tm, LANES), BF16),
            jax.ShapeDtypeStruct((b, nh, nt, HEAD_DIM, tm), BF16),
            jax.ShapeDtypeStruct((b, nh, HEAD_DIM, t), F32),
            jax.ShapeDtypeStruct((b, nh, HEAD_DIM, t), F32),
            jax.ShapeDtypeStruct((b, t, pool_ch), F32),
            jax.ShapeDtypeStruct((b, t, conv_ch), F32),
        ],
        compiler_params=_params("parallel"),
        name="inproj_prompt",
    )(x, *consts)


SOFTPLUS2_CLAMP = 100.0


def _softplus2(z2):
    return jnp.maximum(z2, jnp.log(1.0 + jnp.exp2(jnp.minimum(z2, SOFTPLUS2_CLAMP))) * LOG2E)


def _attn_prompt_kernel(qt_ref, k_ref, vt_ref, tri_ref, o_ref, r_ref, za_ref, zb_ref):
    i = pl.program_id(2)
    heads = qt_ref.shape[0]
    tri = tri_ref[...]
    tk, tq = tri.shape[0], qt_ref.shape[-1]

    def scores(j, z_ref):
        for g in range(heads):
            z_ref[g] = jnp.dot(k_ref[g, j], qt_ref[g], preferred_element_type=F32)

    def step(j, zc_ref, zn_ref, diagonal=False):
        scores(jnp.maximum(j - 1, 0), zn_ref)
        if diagonal:
            vis = lax.broadcasted_iota(jnp.int32, (tk, tq), 0) < lax.broadcasted_iota(jnp.int32, (tk, tq), 1)
            sps = [jnp.where(vis, _softplus2(zc_ref[g]), 0.0) for g in range(heads)]
        else:
            sps = [_softplus2(zc_ref[g]) for g in range(heads)]
        ss = [jnp.dot(tri, sp.astype(BF16), preferred_element_type=F32) for sp in sps]
        for g in range(heads):
            w = jnp.exp2(zc_ref[g] - ss[g])
            if diagonal:
                w = jnp.where(vis, w, 0.0)
            pv = jnp.dot(vt_ref[g, j], w.astype(BF16), preferred_element_type=F32)
            if diagonal:
                o_ref[g] = pv
                r_ref[g] = ss[g][0:1, :]
            else:
                r = r_ref[g]
                o_ref[g] += pv * jnp.exp2(-r)
                r_ref[g] = r + ss[g][0:1, :]

    scores(i, za_ref)
    step(i, za_ref, zb_ref, diagonal=True)

    def body(p, carry):
        j = i - 1 - 2 * p
        step(j, zb_ref, za_ref)
        step(j - 1, za_ref, zb_ref)
        return carry

    lax.fori_loop(0, i // 2, body, 0)

    @pl.when(i % 2 == 1)
    def _():
        step(0, zb_ref, za_ref)


def _attn_prompt(qt, k_hm, vt_hm, tri, *, tq, heads):
    b, h, da, t = qt.shape
    nk, tk = k_hm.shape[2], k_hm.shape[3]
    dh = vt_hm.shape[3]
    return pl.pallas_call(
        _attn_prompt_kernel,
        grid=(b, h // heads, t // tq),
        in_specs=[
            pl.BlockSpec((None, heads, da, tq), lambda bi, hi, i: (bi, hi, 0, i)),
            pl.BlockSpec((None, heads, nk, tk, da), lambda bi, hi, i: (bi, hi, 0, 0, 0), pipeline_mode=pl.Buffered(1)),
            pl.BlockSpec((None, heads, nk, dh, tk), lambda bi, hi, i: (bi, hi, 0, 0, 0), pipeline_mode=pl.Buffered(1)),
            _const_spec((tk, tk)),
        ],
        out_specs=pl.BlockSpec((None, heads, dh, tq), lambda bi, hi, i: (bi, hi, 0, i)),
        out_shape=jax.ShapeDtypeStruct((b, h, dh, t), F32),
        scratch_shapes=[pltpu.VMEM((heads, 1, tq), F32)] + [pltpu.VMEM((heads, tk, tq), F32)] * 2,
        compiler_params=_params("parallel", "parallel", "arbitrary"),
        name="attn_prompt",
    )(qt, k_hm, vt_hm, tri)


def _attn_decode_kernel(pt_ref, q_ref, bias_ref, tri_ref, *refs, pages):
    del pt_ref
    k_refs, v_refs = refs[:pages], refs[pages:2 * pages]
    o_ref, acc_ref, r_ref = refs[2 * pages:]
    j = pl.program_id(1)

    @pl.when(j == 0)
    def _():
        acc_ref[...] = jnp.zeros_like(acc_ref)
        r_ref[...] = jnp.zeros_like(r_ref)

    q = q_ref[...]
    bias = bias_ref[...]
    tri = tri_ref[...]
    r = r_ref[...]
    for p in reversed(range(pages)):
        z = jnp.sum(k_refs[p][...] * q, axis=1) + bias
        s = jnp.dot(_softplus2(z).astype(BF16), tri, preferred_element_type=F32)
        w = jnp.exp2(z - s - r)
        acc_ref[...] += w[:, None, :] * v_refs[p][...]
        r = r + s[:, 0:1]
    r_ref[...] = r

    @pl.when(j == pl.num_programs(1) - 1)
    def _():
        o_ref[...] = jnp.sum(acc_ref[...], axis=-1)


def _attn_decode(page_table, q_rep, bias_rep, tri, cache_kt, cache_vt, *, layer, pages):
    db, n_pages = page_table.shape
    nh, dh, page = cache_kt.shape[2:]
    groups = n_pages // pages

    def page_spec(p):
        return pl.BlockSpec(
            (None, None, nh, dh, page),
            lambda b, j, pt: (layer, pt[b, (groups - 1 - j) * pages + p], 0, 0, 0))

    grid_spec = pltpu.PrefetchScalarGridSpec(
        num_scalar_prefetch=1,
        grid=(db, groups),
        in_specs=[
            pl.BlockSpec((None, nh, dh, page), lambda b, j, pt: (b, 0, 0, 0)),
            pl.BlockSpec((nh, page), lambda b, j, pt: (0, 0)),
            pl.BlockSpec((page, page), lambda b, j, pt: (0, 0)),
        ] + [page_spec(p) for p in range(pages)] * 2,
        out_specs=pl.BlockSpec((None, nh, dh), lambda b, j, pt: (b, 0, 0)),
        scratch_shapes=[pltpu.VMEM((nh, dh, page), F32), pltpu.VMEM((nh, 1), F32)],
    )
    return pl.pallas_call(
        functools.partial(_attn_decode_kernel, pages=pages),
        grid_spec=grid_spec,
        out_shape=jax.ShapeDtypeStruct((db, nh, dh), F32),
        compiler_params=_params("parallel", "arbitrary"),
        name="attn_decode",
    )(page_table, q_rep, bias_rep, tri, *([cache_kt] * pages), *([cache_vt] * pages))


def _conv_tail(y, lng, lnb, pww, pwb):
    mu = jnp.mean(y, axis=-1, keepdims=True)
    yc = y - mu
    var = jnp.mean(yc * yc, axis=-1, keepdims=True)
    yn = yc * lax.rsqrt(var + EPS) * lng + lnb
    yn = (yn * jax.nn.sigmoid(yn)).astype(BF16)
    return jnp.dot(yn, pww, preferred_element_type=F32) + pwb


def _pool_window_of_lane(shape):
    lane = lax.broadcasted_iota(jnp.int32, shape, len(shape) - 1)
    gw = shape[-1] // len(POOL_WINDOWS)
    win = jnp.full(shape, POOL_WINDOWS[-1], jnp.int32)
    for g in reversed(range(len(POOL_WINDOWS) - 1)):
        win = jnp.where(lane < (g + 1) * gw, POOL_WINDOWS[g], win)
    return win


HALO = 32


def _mix_prompt_kernel(u_ref, up_ref, h_ref, hp_ref, poolw_ref, pscale_ref, dww_ref, dwb_ref,
                       lng_ref, lnb_ref, pww_ref, pwb_ref, pool_o, conv_o, uext, hext, ush, hsh):
    i = pl.program_id(1)
    tt, c = u_ref.shape

    @pl.when(i == 0)
    def _():
        uext[0:HALO, :] = jnp.zeros((HALO, c), F32)
        hext[0:HALO, :] = jnp.zeros((HALO, c), F32)

    @pl.when(i > 0)
    def _():
        uext[0:HALO, :] = up_ref[...]
        hext[0:HALO, :] = hp_ref[...]

    u = u_ref[...]
    uext[HALO:, :] = u
    hext[HALO:, :] = h_ref[...]

    for off in range(1, SUBLANES):
        ush[off - 1] = uext[pl.ds(off, HALO + tt - SUBLANES), :]
        hsh[off - 1] = hext[pl.ds(off, HALO + tt - SUBLANES), :]

    def window(ext, shifted, r0):
        a, off = divmod(r0, SUBLANES)
        if off == 0:
            return ext[pl.ds(r0, tt), :]
        return shifted[off - 1, pl.ds(a * SUBLANES, tt), :]

    sums = []
    run = u
    k = 1
    for wnd in POOL_WINDOWS:
        while k < wnd:
            run = run + window(uext, ush, HALO - k)
            k += 1
        sums.append(run)
    win = _pool_window_of_lane((tt, c))
    ssel = sums[-1]
    for g in reversed(range(len(POOL_WINDOWS) - 1)):
        ssel = jnp.where(win == POOL_WINDOWS[g], sums[g], ssel)
    pos = i * tt + lax.broadcasted_iota(jnp.int32, (tt, c), 0)
    cnt = jnp.minimum(win, pos + 1).astype(F32)
    d = (ssel / cnt - u).astype(BF16)
    pool_o[...] = (jnp.dot(d, poolw_ref[...], preferred_element_type=F32) * pscale_ref[...]).astype(BF16)

    y = jnp.zeros((tt, c), F32) + dwb_ref[...]
    for tap in range(CONV_TAPS):
        y = y + window(hext, hsh, HALO - CONV_HIST + tap) * dww_ref[tap:tap + 1, :]
    conv_o[...] = _conv_tail(y, lng_ref[...], lnb_ref[...], pww_ref[...], pwb_ref[...]).astype(BF16)


def _mix_prompt(u, h, poolw_bd, pscale, dww, dwb, lng, lnb, pww, pwb, *, tt):
    b, t, c = u.shape
    r = tt // HALO
    cur = pl.BlockSpec((None, tt, c), lambda bi, i: (bi, i, 0))
    prev = pl.BlockSpec((None, HALO, c), lambda bi, i: (bi, jnp.maximum(i * r - 1, 0), 0))
    vec = _const_spec((1, c))
    return pl.pallas_call(
        _mix_prompt_kernel,
        grid=(b, t // tt),
        in_specs=[cur, prev, cur, prev, _const_spec((c, c)), vec, _const_spec(dww.shape), vec,
                  vec, vec, _const_spec((c, c)), vec],
        out_specs=[cur, cur],
        out_shape=[jax.ShapeDtypeStruct((b, t, c), BF16)] * 2,
        scratch_shapes=[pltpu.VMEM((HALO + tt, c), F32)] * 2
        + [pltpu.VMEM((SUBLANES - 1, HALO + tt - SUBLANES, c), F32)] * 2,
        compiler_params=_params("parallel", "arbitrary"),
        name="mix_prompt",
    )(u, u, h, h, poolw_bd, pscale, dww, dwb, lng, lnb, pww, pwb)


def _mix_decode_kernel(u_ref, sp_ref, h_ref, sc_ref, poolw_ref, pscale_ref, dww_ref, dwb_ref,
                       lng_ref, lnb_ref, pww_ref, pwb_ref, pool_o, conv_o, *, start_pos):
    u = u_ref[...]
    sums = []
    run = u
    k = 1
    for wnd in POOL_WINDOWS:
        while k < wnd:
            run = run + sp_ref[POOL_HIST - k]
            k += 1
        sums.append(run)
    win = _pool_window_of_lane(u.shape)
    ssel = sums[-1]
    for g in reversed(range(len(POOL_WINDOWS) - 1)):
        ssel = jnp.where(win == POOL_WINDOWS[g], sums[g], ssel)
    cnt = jnp.minimum(win, start_pos + 1).astype(F32)
    d = (ssel / cnt - u).astype(BF16)
    pool_o[...] = (jnp.dot(d, poolw_ref[...], preferred_element_type=F32) * pscale_ref[...]).astype(BF16)

    y = h_ref[...] * dww_ref[CONV_HIST:CONV_TAPS, :] + dwb_ref[...]
    for tap in range(CONV_HIST):
        y = y + sc_ref[tap] * dww_ref[tap:tap + 1, :]
    conv_o[...] = _conv_tail(y, lng_ref[...], lnb_ref[...], pww_ref[...], pwb_ref[...]).astype(BF16)


def _mix_decode(u, state_pool_tm, h, state_conv_tm, poolw_bd, pscale, dww, dwb, lng, lnb, pww, pwb, *, start_pos):
    db, c = u.shape
    args = (u, state_pool_tm, h, state_conv_tm, poolw_bd, pscale, dww, dwb, lng, lnb, pww, pwb)
    return pl.pallas_call(
        functools.partial(_mix_decode_kernel, start_pos=start_pos),
        grid=(1,),
        in_specs=[_const_spec(a.shape) for a in args],
        out_specs=[_const_spec((db, c))] * 2,
        out_shape=[jax.ShapeDtypeStruct((db, c), BF16)] * 2,
        compiler_params=_params("arbitrary"),
        name="mix_decode",
    )(*args)


def _merge_ffn_kernel(x_ref, att_ref, pool_ref, conv_ref, wo_ref, g_ref, wup_ref, wdn_ref, o_ref, *, ff_chunk):
    cp = pool_ref.shape[1]
    if len(att_ref.shape) == 3:
        nh, dh, rows = att_ref.shape
        att = att_ref[...].reshape(nh * dh, rows).T.astype(BF16)
    else:
        att = att_ref[...]
    ca = att.shape[1]
    m = jnp.dot(att, wo_ref[0:ca, :], preferred_element_type=F32)
    m = m + jnp.dot(pool_ref[...], wo_ref[ca:ca + cp, :], preferred_element_type=F32)
    m = m + jnp.dot(conv_ref[...], wo_ref[ca + cp:, :], preferred_element_type=F32)
    x1 = x_ref[...] + m
    ms = jnp.mean(x1 * x1, axis=-1, keepdims=True)
    xn = (x1 * lax.rsqrt(ms + EPS) * g_ref[...]).astype(BF16)
    out = x1
    d_ff = wup_ref.shape[1]
    for c0 in range(0, d_ff, ff_chunk):
        hmid = jnp.dot(xn, wup_ref[:, c0:c0 + ff_chunk], preferred_element_type=F32)
        hmid = jnp.square(jnp.maximum(hmid, 0.0)).astype(BF16)
        out = out + jnp.dot(hmid, wdn_ref[c0:c0 + ff_chunk, :], preferred_element_type=F32)
    o_ref[...] = out


def _merge_ffn(x, att, pool, conv, wo_bf, g, wup_bf, wdn_bf, *, tm, ff_chunk):
    m, d = x.shape
    row = lambda c: pl.BlockSpec((tm, c), lambda i: (i, 0))
    single = lambda shape: pl.BlockSpec(shape, lambda i: (0, 0), pipeline_mode=pl.Buffered(1))
    if att.ndim == 4:
        nt = att.shape[3] // tm
        att_spec = pl.BlockSpec((None,) + att.shape[1:3] + (tm,), lambda i: (i // nt, 0, 0, i % nt))
    else:
        att_spec = row(att.shape[1])
    return pl.pallas_call(
        functools.partial(_merge_ffn_kernel, ff_chunk=ff_chunk),
        grid=(m // tm,),
        in_specs=[row(d), att_spec, row(pool.shape[1]), row(conv.shape[1]),
                  single(wo_bf.shape), single((1, d)), single(wup_bf.shape), single(wdn_bf.shape)],
        out_specs=row(d),
        out_shape=jax.ShapeDtypeStruct((m, d), F32),
        compiler_params=_params("parallel"),
        name="merge_ffn",
    )(x, att, pool, conv, wo_bf, g, wup_bf, wdn_bf)


def _block_diag(blocks):
    g, a, b = blocks.shape
    eye = jnp.eye(g, dtype=blocks.dtype)
    return (eye[:, None, :, None] * blocks[:, :, None, :]).reshape(g * a, g * b)


def _reverse_tri(n):
    return (jnp.arange(n)[None, :] >= jnp.arange(n)[:, None]).astype(BF16)


ATTN_TQ = 256
ATTN_HEADS_PER_STEP = 8
PROMPT_TM = 256
MIX_TT = 512
FF_CHUNK = 1024
DECODE_PAGES = 32


def kernel(x_prompt, x_sample, cache_k, cache_v, state_pool, state_conv, page_table, norm_mix_g, w_in, q_norm_g, k_norm_g, sb_bias, pool_w, pool_scale, conv_dw_w, conv_dw_b, conv_ln_g, conv_ln_b, conv_pw_w, conv_pw_b, w_out, norm_ffn_g, w_ffn_up, w_ffn_down):
    b, t, d = x_prompt.shape
    db, ts, _ = x_sample.shape
    assert ts == 1
    assert PROMPT_TM == ATTN_TQ
    depth = w_in.shape[0]
    n_heads = sb_bias.shape[1]
    attn_ch = n_heads * HEAD_DIM
    pool_ch = pool_scale.shape[1]
    conv_ch = conv_dw_b.shape[1]
    past_len = page_table.shape[1] * PAGE_SIZE

    xp = x_prompt.reshape(b * t, d)
    xs = x_sample.reshape(db, d)

    head_mean = _block_diag(jnp.full((n_heads, HEAD_DIM, HEAD_DIM), 1.0 / HEAD_DIM, BF16))
    tri_prompt = _reverse_tri(ATTN_TQ)
    tri_page = _reverse_tri(PAGE_SIZE).T
    cache_kt = cache_k.transpose(0, 1, 3, 4, 2)
    cache_vt = cache_v.transpose(0, 1, 3, 4, 2)
    ch = jnp.arange(attn_ch)
    place = (jnp.arange(n_heads * LANES)[None, :] == ((ch // HEAD_DIM) * LANES + ch % HEAD_DIM)[:, None]).astype(BF16)
    qpad = jnp.zeros((LANES - HEAD_DIM, PROMPT_TM), BF16).at[0:2, :].set(1.0)
    row2 = lambda v: v.reshape(1, -1)

    outs = {n: [] for n in ("kp", "vp", "ks", "vs", "pp", "ps", "cp", "cs")}
    for l in range(depth):
        poolw_bd = _block_diag(pool_w[l]).astype(BF16)
        mix_w = (poolw_bd, row2(pool_scale[l]), conv_dw_w[l], row2(conv_dw_b[l]),
                 row2(conv_ln_g[l]), row2(conv_ln_b[l]), conv_pw_w[l].astype(BF16), row2(conv_pw_b[l]))
        in_w = (row2(norm_mix_g[l]), w_in[l].astype(BF16), row2(jnp.tile(q_norm_g[l], n_heads)),
                row2(jnp.tile(k_norm_g[l], n_heads)), head_mean)
        dims = dict(attn_ch=attn_ch, pool_ch=pool_ch, conv_ch=conv_ch)
        ffn_w = (w_out[l].astype(BF16), row2(norm_ffn_g[l]), w_ffn_up[l].astype(BF16), w_ffn_down[l].astype(BF16))
        bias2 = sb_bias[l] * LOG2E
        bias_hi = bias2.astype(BF16)
        bias_lo = (bias2 - bias_hi.astype(F32)).astype(BF16)
        kpad = jnp.zeros((n_heads, LANES), F32).at[:, HEAD_DIM].set(bias_hi.astype(F32))
        kpad = kpad.at[:, HEAD_DIM + 1].set(bias_lo.astype(F32)).reshape(1, n_heads * LANES)

        qt, k_aug, vt, kpt, vpt, up, hp = _inproj_prompt(x_prompt if l == 0 else xp.reshape(b, t, d), *in_w,
                                                         place, kpad, qpad, tm=PROMPT_TM, **dims)
        qs, ksf, vsf, us, hs = _inproj_decode(xs, *in_w, **dims)

        att_t = _attn_prompt(qt, k_aug, vt, tri_prompt, tq=ATTN_TQ, heads=ATTN_HEADS_PER_STEP)

        q_rep = jnp.broadcast_to(qs.reshape(db, n_heads, HEAD_DIM, 1), (db, n_heads, HEAD_DIM, PAGE_SIZE))
        bias_rep = jnp.broadcast_to(bias2[:, None], (n_heads, PAGE_SIZE))
        att_s = _attn_decode(page_table, q_rep, bias_rep, tri_page, cache_kt, cache_vt,
                             layer=l, pages=DECODE_PAGES).reshape(db, attn_ch).astype(BF16)

        pool_p, conv_p = _mix_prompt(up, hp, *mix_w, tt=MIX_TT)
        pool_s, conv_s = _mix_decode(us, state_pool[l].transpose(1, 0, 2), hs, state_conv[l].transpose(1, 0, 2),
                                     *mix_w, start_pos=past_len)

        xp = _merge_ffn(xp, att_t, pool_p.reshape(b * t, pool_ch), conv_p.reshape(b * t, conv_ch), *ffn_w,
                        tm=PROMPT_TM, ff_chunk=FF_CHUNK)
        xs = _merge_ffn(xs, att_s, pool_s, conv_s, *ffn_w, tm=db, ff_chunk=FF_CHUNK)

        outs["kp"].append(kpt.transpose(0, 3, 1, 2))
        outs["vp"].append(vpt.transpose(0, 3, 1, 2))
        outs["ks"].append(ksf.reshape(db, ts, n_heads, HEAD_DIM))
        outs["vs"].append(vsf.reshape(db, ts, n_heads, HEAD_DIM))
        outs["pp"].append(up[:, t - POOL_HIST:])
        outs["ps"].append(jnp.concatenate([state_pool[l][:, 1:], us[:, None, :]], axis=1))
        outs["cp"].append(hp[:, t - CONV_HIST:])
        outs["cs"].append(jnp.concatenate([state_conv[l][:, 1:], hs[:, None, :]], axis=1))

    return (xp.reshape(b, t, d), xs.reshape(db, ts, d),
            jnp.stack(outs["kp"]), jnp.stack(outs["vp"]), jnp.stack(outs["ks"]), jnp.stack(outs["vs"]),
            jnp.stack(outs["pp"]), jnp.stack(outs["ps"]), jnp.stack(outs["cp"]), jnp.stack(outs["cs"]))
```

```python
import functools

import jax
import jax.numpy as jnp
from jax import lax
from jax.experimental import pallas as pl
from jax.experimental.pallas import tpu as pltpu

F32 = jnp.float32
BF16 = jnp.bfloat16

EPS = 1e-6
LOG2E = 1.4426950408889634
HEAD_DIM = 64
POOL_WINDOWS = (2, 4, 8, 16)
POOL_HIST = max(POOL_WINDOWS) - 1
CONV_TAPS = 31
CONV_HIST = CONV_TAPS - 1
PAGE_SIZE = 128

V7X_VMEM_LIMIT_BYTES = 56 * 1024 * 1024
LANES = 128
SUBLANES = 8


def _params(*sem):
    return pltpu.CompilerParams(dimension_semantics=sem, vmem_limit_bytes=V7X_VMEM_LIMIT_BYTES)


def _const_spec(shape):
    nd = len(shape)
    return pl.BlockSpec(shape, lambda *_: (0,) * nd)


def _inproj_math(x, g, w, qg, kg, hm, *, attn_ch, pool_ch, conv_ch):
    ms = jnp.mean(x * x, axis=-1, keepdims=True)
    xn = (x * lax.rsqrt(ms + EPS) * g).astype(BF16)
    z = jnp.dot(xn, w, preferred_element_type=F32)
    c0, c1, c2 = attn_ch, 2 * attn_ch, 3 * attn_ch
    c3 = c2 + pool_ch
    c4 = c3 + conv_ch
    q, k, v = z[:, :c0], z[:, c0:c1], z[:, c1:c2]
    u, a, gate = z[:, c2:c3], z[:, c3:c4], z[:, c4:]

    def head_norm(t, gain):
        msq = jnp.dot((t * t).astype(BF16), hm, preferred_element_type=F32)
        return t * lax.rsqrt(msq + EPS) * gain

    q_scaled = head_norm(q, qg) * (HEAD_DIM ** -0.5 * LOG2E)
    return q_scaled, head_norm(k, kg), v, u, a * jax.nn.sigmoid(gate)


def _inproj_decode_kernel(x_ref, g_ref, w_ref, qg_ref, kg_ref, hm_ref,
                          q_ref, k_ref, v_ref, u_ref, h_ref, **dims):
    q, k, v, u, h = _inproj_math(x_ref[...], g_ref[...], w_ref[...], qg_ref[...], kg_ref[...], hm_ref[...], **dims)
    q_ref[...] = q
    k_ref[...] = k
    v_ref[...] = v
    u_ref[...] = u
    h_ref[...] = h


def _inproj_decode(x, g, w_bf, qg_t, kg_t, head_mean, *, attn_ch, pool_ch, conv_ch):
    m, d = x.shape
    widths = (attn_ch, attn_ch, attn_ch, pool_ch, conv_ch)
    args = (x, g, w_bf, qg_t, kg_t, head_mean)
    return pl.pallas_call(
        functools.partial(_inproj_decode_kernel, attn_ch=attn_ch, pool_ch=pool_ch, conv_ch=conv_ch),
        grid=(1,),
        in_specs=[_const_spec(a.shape) for a in args],
        out_specs=[_const_spec((m, c)) for c in widths],
        out_shape=[jax.ShapeDtypeStruct((m, c), F32) for c in widths],
        compiler_params=_params("arbitrary"),
        name="inproj_decode",
    )(*args)


def _inproj_prompt_kernel(x_ref, g_ref, w_ref, qg_ref, kg_ref, hm_ref, place_ref, kpad_ref, qpad_ref, *rest, **dims):
    qt_ref, ka_ref, vt_ref, kf_ref, vf_ref, u_ref, h_ref = rest[-7:]
    q, k, v, u, h = _inproj_math(x_ref[...], g_ref[...], w_ref[...], qg_ref[...], kg_ref[...], hm_ref[...], **dims)
    u_ref[...] = u
    h_ref[...] = h
    ka = (jnp.dot(k.astype(BF16), place_ref[...], preferred_element_type=F32) + kpad_ref[...]).astype(BF16)
    qt, kt, vt = q.T, k.T, v.T
    for hd in range(qt_ref.shape[0]):
        rows = slice(hd * HEAD_DIM, (hd + 1) * HEAD_DIM)
        qt_ref[hd, 0:HEAD_DIM, :] = qt[rows].astype(BF16)
        qt_ref[hd, HEAD_DIM:, :] = qpad_ref[...]
        ka_ref[hd] = ka[:, hd * LANES:(hd + 1) * LANES]
        vt_ref[hd] = vt[rows].astype(BF16)
        kf_ref[hd] = kt[rows]
        vf_ref[hd] = vt[rows]


def _inproj_prompt(x, g, w_bf, qg_t, kg_t, head_mean, place, kpad, qpad, kv_all, *,
                   layer, depth, tm, attn_ch, pool_ch, conv_ch):
    b, t, d = x.shape
    nt = t // tm
    nh = attn_ch // HEAD_DIM
    tok_minor = pl.BlockSpec((None, None, nh, HEAD_DIM, tm), lambda i: (layer, i // nt, 0, 0, i % nt))
    row = lambda c: pl.BlockSpec((None, tm, c), lambda i: (i // nt, i % nt, 0))
    consts = (g, w_bf, qg_t, kg_t, head_mean, place, kpad, qpad)
    carried = () if kv_all is None else tuple(kv_all)
    first_carried = 1 + len(consts)
    kv_shape = jax.ShapeDtypeStruct((depth, b, nh, HEAD_DIM, t), F32)
    return pl.pallas_call(
        functools.partial(_inproj_prompt_kernel, attn_ch=attn_ch, pool_ch=pool_ch, conv_ch=conv_ch),
        grid=(b * nt,),
        in_specs=[row(d)] + [_const_spec(a.shape) for a in consts] + [pl.BlockSpec(memory_space=pl.ANY)] * len(carried),
        out_specs=[
            pl.BlockSpec((None, nh, LANES, tm), lambda i: (i // nt, 0, 0, i % nt)),
            pl.BlockSpec((None, nh, None, tm, LANES), lambda i: (i // nt, 0, i % nt, 0, 0)),
            pl.BlockSpec((None, nh, None, HEAD_DIM, tm), lambda i: (i // nt, 0, i % nt, 0, 0)),
            tok_minor, tok_minor, row(pool_ch), row(conv_ch)],
        out_shape=[
            jax.ShapeDtypeStruct((b, nh, LANES, t), BF16),
            jax.ShapeDtypeStruct((b, nh, nt, tm, LANES), BF16),
            jax.ShapeDtypeStruct((b, nh, nt, HEAD_DIM, tm), BF16),
            kv_shape,
            kv_shape,
            jax.ShapeDtypeStruct((b, t, pool_ch), F32),
            jax.ShapeDtypeStruct((b, t, conv_ch), F32),
        ],
        input_output_aliases={first_carried + n: 3 + n for n in range(len(carried))},
        compiler_params=_params("parallel"),
        name="inproj_prompt",
    )(x, *consts, *carried)


SOFTPLUS2_CLAMP = 100.0


def _softplus2(z2):
    return jnp.maximum(z2, jnp.log(1.0 + jnp.exp2(jnp.minimum(z2, SOFTPLUS2_CLAMP))) * LOG2E)


def _attn_prompt_kernel(qt_ref, k_ref, vt_ref, tri_ref, o_ref, r_ref, za_ref, zb_ref):
    i = pl.program_id(2)
    heads = qt_ref.shape[0]
    tri = tri_ref[...]
    tk, tq = tri.shape[0], qt_ref.shape[-1]

    def scores(j, z_ref):
        for g in range(heads):
            z_ref[g] = jnp.dot(k_ref[g, j], qt_ref[g], preferred_element_type=F32)

    def step(j, zc_ref, zn_ref, diagonal=False):
        scores(jnp.maximum(j - 1, 0), zn_ref)
        if diagonal:
            vis = lax.broadcasted_iota(jnp.int32, (tk, tq), 0) < lax.broadcasted_iota(jnp.int32, (tk, tq), 1)
            sps = [jnp.where(vis, _softplus2(zc_ref[g]), 0.0) for g in range(heads)]
        else:
            sps = [_softplus2(zc_ref[g]) for g in range(heads)]
        ss = [jnp.dot(tri, sp.astype(BF16), preferred_element_type=F32) for sp in sps]
        for g in range(heads):
            w = jnp.exp2(zc_ref[g] - ss[g])
            if diagonal:
                w = jnp.where(vis, w, 0.0)
            pv = jnp.dot(vt_ref[g, j], w.astype(BF16), preferred_element_type=F32)
            if diagonal:
                o_ref[g] = pv
                r_ref[g] = ss[g][0:1, :]
            else:
                r = r_ref[g]
                o_ref[g] += pv * jnp.exp2(-r)
                r_ref[g] = r + ss[g][0:1, :]

    scores(i, za_ref)
    step(i, za_ref, zb_ref, diagonal=True)

    def body(p, carry):
        j = i - 1 - 2 * p
        step(j, zb_ref, za_ref)
        step(j - 1, za_ref, zb_ref)
        return carry

    lax.fori_loop(0, i // 2, body, 0)

    @pl.when(i % 2 == 1)
    def _():
        step(0, zb_ref, za_ref)


def _attn_prompt(qt, k_hm, vt_hm, tri, *, tq, heads):
    b, h, da, t = qt.shape
    nk, tk = k_hm.shape[2], k_hm.shape[3]
    dh = vt_hm.shape[3]
    return pl.pallas_call(
        _attn_prompt_kernel,
        grid=(b, h // heads, t // tq),
        in_specs=[
            pl.BlockSpec((None, heads, da, tq), lambda bi, hi, i: (bi, hi, 0, i)),
            pl.BlockSpec((None, heads, nk, tk, da), lambda bi, hi, i: (bi, hi, 0, 0, 0), pipeline_mode=pl.Buffered(1)),
            pl.BlockSpec((None, heads, nk, dh, tk), lambda bi, hi, i: (bi, hi, 0, 0, 0), pipeline_mode=pl.Buffered(1)),
            _const_spec((tk, tk)),
        ],
        out_specs=pl.BlockSpec((None, heads, dh, tq), lambda bi, hi, i: (bi, hi, 0, i)),
        out_shape=jax.ShapeDtypeStruct((b, h, dh, t), F32),
        scratch_shapes=[pltpu.VMEM((heads, 1, tq), F32)] + [pltpu.VMEM((heads, tk, tq), F32)] * 2,
        compiler_params=_params("parallel", "parallel", "arbitrary"),
        name="attn_prompt",
    )(qt, k_hm, vt_hm, tri)


def _attn_decode_kernel(pt_ref, q_ref, bias_ref, tri_ref, *refs, pages):
    del pt_ref
    k_refs, v_refs = refs[:pages], refs[pages:2 * pages]
    o_ref, acc_ref, r_ref = refs[2 * pages:]
    j = pl.program_id(1)

    @pl.when(j == 0)
    def _():
        acc_ref[...] = jnp.zeros_like(acc_ref)
        r_ref[...] = jnp.zeros_like(r_ref)

    q = q_ref[...]
    bias = bias_ref[...]
    tri = tri_ref[...]
    r = r_ref[...]
    for p in reversed(range(pages)):
        z = jnp.sum(k_refs[p][...] * q, axis=1) + bias
        s = jnp.dot(_softplus2(z).astype(BF16), tri, preferred_element_type=F32)
        w = jnp.exp2(z - s - r)
        acc_ref[...] += w[:, None, :] * v_refs[p][...]
        r = r + s[:, 0:1]
    r_ref[...] = r

    @pl.when(j == pl.num_programs(1) - 1)
    def _():
        o_ref[...] = jnp.sum(acc_ref[...], axis=-1)


def _attn_decode(page_table, q_rep, bias_rep, tri, cache_kt, cache_vt, *, layer, pages):
    db, n_pages = page_table.shape
    nh, dh, page = cache_kt.shape[2:]
    groups = n_pages // pages

    def page_spec(p):
        return pl.BlockSpec(
            (None, None, nh, dh, page),
            lambda b, j, pt: (layer, pt[b, (groups - 1 - j) * pages + p], 0, 0, 0))

    grid_spec = pltpu.PrefetchScalarGridSpec(
        num_scalar_prefetch=1,
        grid=(db, groups),
        in_specs=[
            pl.BlockSpec((None, nh, dh, page), lambda b, j, pt: (b, 0, 0, 0)),
            pl.BlockSpec((nh, page), lambda b, j, pt: (0, 0)),
            pl.BlockSpec((page, page), lambda b, j, pt: (0, 0)),
        ] + [page_spec(p) for p in range(pages)] * 2,
        out_specs=pl.BlockSpec((None, nh, dh), lambda b, j, pt: (b, 0, 0)),
        scratch_shapes=[pltpu.VMEM((nh, dh, page), F32), pltpu.VMEM((nh, 1), F32)],
    )
    return pl.pallas_call(
        functools.partial(_attn_decode_kernel, pages=pages),
        grid_spec=grid_spec,
        out_shape=jax.ShapeDtypeStruct((db, nh, dh), F32),
        compiler_params=_params("parallel", "arbitrary"),
        name="attn_decode",
    )(page_table, q_rep, bias_rep, tri, *([cache_kt] * pages), *([cache_vt] * pages))


def _conv_tail(y, lng, lnb, pww, pwb):
    mu = jnp.mean(y, axis=-1, keepdims=True)
    yc = y - mu
    var = jnp.mean(yc * yc, axis=-1, keepdims=True)
    yn = yc * lax.rsqrt(var + EPS) * lng + lnb
    yn = (yn * jax.nn.sigmoid(yn)).astype(BF16)
    return jnp.dot(yn, pww, preferred_element_type=F32) + pwb


def _pool_window_of_lane(shape):
    lane = lax.broadcasted_iota(jnp.int32, shape, len(shape) - 1)
    gw = shape[-1] // len(POOL_WINDOWS)
    win = jnp.full(shape, POOL_WINDOWS[-1], jnp.int32)
    for g in reversed(range(len(POOL_WINDOWS) - 1)):
        win = jnp.where(lane < (g + 1) * gw, POOL_WINDOWS[g], win)
    return win


HALO = 32


def _mix_prompt_kernel(u_ref, up_ref, h_ref, hp_ref, poolw_ref, pscale_ref, dww_ref, dwb_ref,
                       lng_ref, lnb_ref, pww_ref, pwb_ref, pool_o, conv_o, uext, hext, ush, hsh):
    i = pl.program_id(1)
    tt, c = u_ref.shape

    @pl.when(i == 0)
    def _():
        uext[0:HALO, :] = jnp.zeros((HALO, c), F32)
        hext[0:HALO, :] = jnp.zeros((HALO, c), F32)

    @pl.when(i > 0)
    def _():
        uext[0:HALO, :] = up_ref[...]
        hext[0:HALO, :] = hp_ref[...]

    u = u_ref[...]
    uext[HALO:, :] = u
    hext[HALO:, :] = h_ref[...]

    for off in range(1, SUBLANES):
        ush[off - 1] = uext[pl.ds(off, HALO + tt - SUBLANES), :]
        hsh[off - 1] = hext[pl.ds(off, HALO + tt - SUBLANES), :]

    def window(ext, shifted, r0):
        a, off = divmod(r0, SUBLANES)
        if off == 0:
            return ext[pl.ds(r0, tt), :]
        return shifted[off - 1, pl.ds(a * SUBLANES, tt), :]

    sums = []
    run = u
    k = 1
    for wnd in POOL_WINDOWS:
        while k < wnd:
            run = run + window(uext, ush, HALO - k)
            k += 1
        sums.append(run)
    win = _pool_window_of_lane((tt, c))
    ssel = sums[-1]
    for g in reversed(range(len(POOL_WINDOWS) - 1)):
        ssel = jnp.where(win == POOL_WINDOWS[g], sums[g], ssel)
    pos = i * tt + lax.broadcasted_iota(jnp.int32, (tt, c), 0)
    cnt = jnp.minimum(win, pos + 1).astype(F32)
    d = (ssel / cnt - u).astype(BF16)
    pool_o[...] = (jnp.dot(d, poolw_ref[...], preferred_element_type=F32) * pscale_ref[...]).astype(BF16)

    y = jnp.zeros((tt, c), F32) + dwb_ref[...]
    for tap in range(CONV_TAPS):
        y = y + window(hext, hsh, HALO - CONV_HIST + tap) * dww_ref[tap:tap + 1, :]
    conv_o[...] = _conv_tail(y, lng_ref[...], lnb_ref[...], pww_ref[...], pwb_ref[...]).astype(BF16)


def _mix_prompt(u, h, poolw_bd, pscale, dww, dwb, lng, lnb, pww, pwb, *, tt):
    b, t, c = u.shape
    r = tt // HALO
    cur = pl.BlockSpec((None, tt, c), lambda bi, i: (bi, i, 0))
    prev = pl.BlockSpec((None, HALO, c), lambda bi, i: (bi, jnp.maximum(i * r - 1, 0), 0))
    vec = _const_spec((1, c))
    return pl.pallas_call(
        _mix_prompt_kernel,
        grid=(b, t // tt),
        in_specs=[cur, prev, cur, prev, _const_spec((c, c)), vec, _const_spec(dww.shape), vec,
                  vec, vec, _const_spec((c, c)), vec],
        out_specs=[cur, cur],
        out_shape=[jax.ShapeDtypeStruct((b, t, c), BF16)] * 2,
        scratch_shapes=[pltpu.VMEM((HALO + tt, c), F32)] * 2
        + [pltpu.VMEM((SUBLANES - 1, HALO + tt - SUBLANES, c), F32)] * 2,
        compiler_params=_params("parallel", "arbitrary"),
        name="mix_prompt",
    )(u, u, h, h, poolw_bd, pscale, dww, dwb, lng, lnb, pww, pwb)


def _mix_decode_kernel(u_ref, sp_ref, h_ref, sc_ref, poolw_ref, pscale_ref, dww_ref, dwb_ref,
                       lng_ref, lnb_ref, pww_ref, pwb_ref, pool_o, conv_o, *, start_pos):
    u = u_ref[...]
    sums = []
    run = u
    k = 1
    for wnd in POOL_WINDOWS:
        while k < wnd:
            run = run + sp_ref[POOL_HIST - k]
            k += 1
        sums.append(run)
    win = _pool_window_of_lane(u.shape)
    ssel = sums[-1]
    for g in reversed(range(len(POOL_WINDOWS) - 1)):
        ssel = jnp.where(win == POOL_WINDOWS[g], sums[g], ssel)
    cnt = jnp.minimum(win, start_pos + 1).astype(F32)
    d = (ssel / cnt - u).astype(BF16)
    pool_o[...] = (jnp.dot(d, poolw_ref[...], preferred_element_type=F32) * pscale_ref[...]).astype(BF16)

    y = h_ref[...] * dww_ref[CONV_HIST:CONV_TAPS, :] + dwb_ref[...]
    for tap in range(CONV_HIST):
        y = y + sc_ref[tap] * dww_ref[tap:tap + 1, :]
    conv_o[...] = _conv_tail(y, lng_ref[...], lnb_ref[...], pww_ref[...], pwb_ref[...]).astype(BF16)


def _mix_decode(u, state_pool_tm, h, state_conv_tm, poolw_bd, pscale, dww, dwb, lng, lnb, pww, pwb, *, start_pos):
    db, c = u.shape
    args = (u, state_pool_tm, h, state_conv_tm, poolw_bd, pscale, dww, dwb, lng, lnb, pww, pwb)
    return pl.pallas_call(
        functools.partial(_mix_decode_kernel, start_pos=start_pos),
        grid=(1,),
        in_specs=[_const_spec(a.shape) for a in args],
        out_specs=[_const_spec((db, c))] * 2,
        out_shape=[jax.ShapeDtypeStruct((db, c), BF16)] * 2,
        compiler_params=_params("arbitrary"),
        name="mix_decode",
    )(*args)


def _merge_ffn_kernel(x_ref, att_ref, pool_ref, conv_ref, wo_ref, g_ref, wup_ref, wdn_ref, o_ref, *, ff_chunk):
    cp = pool_ref.shape[1]
    if len(att_ref.shape) == 3:
        nh, dh, rows = att_ref.shape
        att = att_ref[...].reshape(nh * dh, rows).T.astype(BF16)
    else:
        att = att_ref[...]
    ca = att.shape[1]
    m = jnp.dot(att, wo_ref[0:ca, :], preferred_element_type=F32)
    m = m + jnp.dot(pool_ref[...], wo_ref[ca:ca + cp, :], preferred_element_type=F32)
    m = m + jnp.dot(conv_ref[...], wo_ref[ca + cp:, :], preferred_element_type=F32)
    x1 = x_ref[...] + m
    ms = jnp.mean(x1 * x1, axis=-1, keepdims=True)
    xn = (x1 * lax.rsqrt(ms + EPS) * g_ref[...]).astype(BF16)
    out = x1
    d_ff = wup_ref.shape[1]
    for c0 in range(0, d_ff, ff_chunk):
        hmid = jnp.dot(xn, wup_ref[:, c0:c0 + ff_chunk], preferred_element_type=F32)
        hmid = jnp.square(jnp.maximum(hmid, 0.0)).astype(BF16)
        out = out + jnp.dot(hmid, wdn_ref[c0:c0 + ff_chunk, :], preferred_element_type=F32)
    o_ref[...] = out


def _merge_ffn(x, att, pool, conv, wo_bf, g, wup_bf, wdn_bf, *, tm, ff_chunk):
    m, d = x.shape
    row = lambda c: pl.BlockSpec((tm, c), lambda i: (i, 0))
    single = lambda shape: pl.BlockSpec(shape, lambda i: (0, 0), pipeline_mode=pl.Buffered(1))
    if att.ndim == 4:
        nt = att.shape[3] // tm
        att_spec = pl.BlockSpec((None,) + att.shape[1:3] + (tm,), lambda i: (i // nt, 0, 0, i % nt))
    else:
        att_spec = row(att.shape[1])
    return pl.pallas_call(
        functools.partial(_merge_ffn_kernel, ff_chunk=ff_chunk),
        grid=(m // tm,),
        in_specs=[row(d), att_spec, row(pool.shape[1]), row(conv.shape[1]),
                  single(wo_bf.shape), single((1, d)), single(wup_bf.shape), single(wdn_bf.shape)],
        out_specs=row(d),
        out_shape=jax.ShapeDtypeStruct((m, d), F32),
        compiler_params=_params("parallel"),
        name="merge_ffn",
    )(x, att, pool, conv, wo_bf, g, wup_bf, wdn_bf)


def _block_diag(blocks):
    g, a, b = blocks.shape
    eye = jnp.eye(g, dtype=blocks.dtype)
    return (eye[:, None, :, None] * blocks[:, :, None, :]).reshape(g * a, g * b)


def _reverse_tri(n):
    return (jnp.arange(n)[None, :] >= jnp.arange(n)[:, None]).astype(BF16)


ATTN_TQ = 256
ATTN_HEADS_PER_STEP = 8
PROMPT_TM = 256
FFN_TM = 512
MIX_TT = 512
FF_CHUNK = 1024
DECODE_PAGES = 32


def kernel(x_prompt, x_sample, cache_k, cache_v, state_pool, state_conv, page_table, norm_mix_g, w_in, q_norm_g, k_norm_g, sb_bias, pool_w, pool_scale, conv_dw_w, conv_dw_b, conv_ln_g, conv_ln_b, conv_pw_w, conv_pw_b, w_out, norm_ffn_g, w_ffn_up, w_ffn_down):
    b, t, d = x_prompt.shape
    db, ts, _ = x_sample.shape
    assert ts == 1
    assert PROMPT_TM == ATTN_TQ
    depth = w_in.shape[0]
    n_heads = sb_bias.shape[1]
    attn_ch = n_heads * HEAD_DIM
    pool_ch = pool_scale.shape[1]
    conv_ch = conv_dw_b.shape[1]
    past_len = page_table.shape[1] * PAGE_SIZE

    xp = x_prompt.reshape(b * t, d)
    xs = x_sample.reshape(db, d)

    head_mean = _block_diag(jnp.full((n_heads, HEAD_DIM, HEAD_DIM), 1.0 / HEAD_DIM, BF16))
    tri_prompt = _reverse_tri(ATTN_TQ)
    tri_page = _reverse_tri(PAGE_SIZE).T
    cache_kt = cache_k.transpose(0, 1, 3, 4, 2)
    cache_vt = cache_v.transpose(0, 1, 3, 4, 2)
    ch = jnp.arange(attn_ch)
    place = (jnp.arange(n_heads * LANES)[None, :] == ((ch // HEAD_DIM) * LANES + ch % HEAD_DIM)[:, None]).astype(BF16)
    qpad = jnp.zeros((LANES - HEAD_DIM, PROMPT_TM), BF16).at[0:2, :].set(1.0)
    row2 = lambda v: v.reshape(1, -1)

    outs = {n: [] for n in ("ks", "vs", "pp", "ps", "cp", "cs")}
    kv_all = None
    for l in range(depth):
        poolw_bd = _block_diag(pool_w[l]).astype(BF16)
        mix_w = (poolw_bd, row2(pool_scale[l]), conv_dw_w[l], row2(conv_dw_b[l]),
                 row2(conv_ln_g[l]), row2(conv_ln_b[l]), conv_pw_w[l].astype(BF16), row2(conv_pw_b[l]))
        in_w = (row2(norm_mix_g[l]), w_in[l].astype(BF16), row2(jnp.tile(q_norm_g[l], n_heads)),
                row2(jnp.tile(k_norm_g[l], n_heads)), head_mean)
        dims = dict(attn_ch=attn_ch, pool_ch=pool_ch, conv_ch=conv_ch)
        ffn_w = (w_out[l].astype(BF16), row2(norm_ffn_g[l]), w_ffn_up[l].astype(BF16), w_ffn_down[l].astype(BF16))
        bias2 = sb_bias[l] * LOG2E
        bias_hi = bias2.astype(BF16)
        bias_lo = (bias2 - bias_hi.astype(F32)).astype(BF16)
        kpad = jnp.zeros((n_heads, LANES), F32).at[:, HEAD_DIM].set(bias_hi.astype(F32))
        kpad = kpad.at[:, HEAD_DIM + 1].set(bias_lo.astype(F32)).reshape(1, n_heads * LANES)

        qt, k_aug, vt, kt_all, vt_all, up, hp = _inproj_prompt(
            x_prompt if l == 0 else xp.reshape(b, t, d), *in_w, place, kpad, qpad, kv_all,
            layer=l, depth=depth, tm=PROMPT_TM, **dims)
        kv_all = (kt_all, vt_all)
        qs, ksf, vsf, us, hs = _inproj_decode(xs, *in_w, **dims)

        att_t = _attn_prompt(qt, k_aug, vt, tri_prompt, tq=ATTN_TQ, heads=ATTN_HEADS_PER_STEP)

        q_rep = jnp.broadcast_to(qs.reshape(db, n_heads, HEAD_DIM, 1), (db, n_heads, HEAD_DIM, PAGE_SIZE))
        bias_rep = jnp.broadcast_to(bias2[:, None], (n_heads, PAGE_SIZE))
        att_s = _attn_decode(page_table, q_rep, bias_rep, tri_page, cache_kt, cache_vt,
                             layer=l, pages=DECODE_PAGES).reshape(db, attn_ch).astype(BF16)

        pool_p, conv_p = _mix_prompt(up, hp, *mix_w, tt=MIX_TT)
        pool_s, conv_s = _mix_decode(us, state_pool[l].transpose(1, 0, 2), hs, state_conv[l].transpose(1, 0, 2),
                                     *mix_w, start_pos=past_len)

        xp = _merge_ffn(xp, att_t, pool_p.reshape(b * t, pool_ch), conv_p.reshape(b * t, conv_ch), *ffn_w,
                        tm=FFN_TM, ff_chunk=FF_CHUNK)
        xs = _merge_ffn(xs, att_s, pool_s, conv_s, *ffn_w, tm=db, ff_chunk=FF_CHUNK)

        outs["ks"].append(ksf.reshape(db, ts, n_heads, HEAD_DIM))
        outs["vs"].append(vsf.reshape(db, ts, n_heads, HEAD_DIM))
        outs["pp"].append(up[:, t - POOL_HIST:])
        outs["ps"].append(jnp.concatenate([state_pool[l][:, 1:], us[:, None, :]], axis=1))
        outs["cp"].append(hp[:, t - CONV_HIST:])
        outs["cs"].append(jnp.concatenate([state_conv[l][:, 1:], hs[:, None, :]], axis=1))

    return (xp.reshape(b, t, d), xs.reshape(db, ts, d),
            kv_all[0].transpose(0, 1, 4, 2, 3), kv_all[1].transpose(0, 1, 4, 2, 3),
            jnp.stack(outs["ks"]), jnp.stack(outs["vs"]),
            jnp.stack(outs["pp"]), jnp.stack(outs["ps"]), jnp.stack(outs["cp"]), jnp.stack(outs["cs"]))
```

```python
import functools

import jax
import jax.numpy as jnp
from jax import lax
from jax.experimental import pallas as pl
from jax.experimental.pallas import tpu as pltpu

F32 = jnp.float32
BF16 = jnp.bfloat16

EPS = 1e-6
LOG2E = 1.4426950408889634
HEAD_DIM = 64
POOL_WINDOWS = (2, 4, 8, 16)
POOL_HIST = max(POOL_WINDOWS) - 1
CONV_TAPS = 31
CONV_HIST = CONV_TAPS - 1
PAGE_SIZE = 128

V7X_VMEM_LIMIT_BYTES = 56 * 1024 * 1024
LANES = 128
SUBLANES = 8


def _params(*sem):
    return pltpu.CompilerParams(dimension_semantics=sem, vmem_limit_bytes=V7X_VMEM_LIMIT_BYTES)


def _const_spec(shape):
    nd = len(shape)
    return pl.BlockSpec(shape, lambda *_: (0,) * nd)


def _inproj_math(x, g, w, qg, kg, hm, *, attn_ch, pool_ch, conv_ch):
    ms = jnp.mean(x * x, axis=-1, keepdims=True)
    xn = (x * lax.rsqrt(ms + EPS) * g).astype(BF16)
    z = jnp.dot(xn, w, preferred_element_type=F32)
    c0, c1, c2 = attn_ch, 2 * attn_ch, 3 * attn_ch
    c3 = c2 + pool_ch
    c4 = c3 + conv_ch
    q, k, v = z[:, :c0], z[:, c0:c1], z[:, c1:c2]
    u, a, gate = z[:, c2:c3], z[:, c3:c4], z[:, c4:]

    def head_norm(t, gain):
        msq = jnp.dot((t * t).astype(BF16), hm, preferred_element_type=F32)
        return t * lax.rsqrt(msq + EPS) * gain

    q_scaled = head_norm(q, qg) * (HEAD_DIM ** -0.5 * LOG2E)
    return q_scaled, head_norm(k, kg), v, u, a * jax.nn.sigmoid(gate)


def _inproj_decode_kernel(x_ref, g_ref, w_ref, qg_ref, kg_ref, hm_ref,
                          q_ref, k_ref, v_ref, u_ref, h_ref, **dims):
    q, k, v, u, h = _inproj_math(x_ref[...], g_ref[...], w_ref[...], qg_ref[...], kg_ref[...], hm_ref[...], **dims)
    q_ref[...] = q
    k_ref[...] = k
    v_ref[...] = v
    u_ref[...] = u
    h_ref[...] = h


def _inproj_decode(x, g, w_bf, qg_t, kg_t, head_mean, *, attn_ch, pool_ch, conv_ch):
    m, d = x.shape
    widths = (attn_ch, attn_ch, attn_ch, pool_ch, conv_ch)
    args = (x, g, w_bf, qg_t, kg_t, head_mean)
    return pl.pallas_call(
        functools.partial(_inproj_decode_kernel, attn_ch=attn_ch, pool_ch=pool_ch, conv_ch=conv_ch),
        grid=(1,),
        in_specs=[_const_spec(a.shape) for a in args],
        out_specs=[_const_spec((m, c)) for c in widths],
        out_shape=[jax.ShapeDtypeStruct((m, c), F32) for c in widths],
        compiler_params=_params("arbitrary"),
        name="inproj_decode",
    )(*args)


def _inproj_prompt_kernel(x_ref, g_ref, w_ref, qg_ref, kg_ref, hm_ref, place_ref, kpad_ref, qpad_ref, *rest, **dims):
    qt_ref, ka_ref, vt_ref, kf_ref, vf_ref, u_ref, h_ref = rest[-7:]
    q, k, v, u, h = _inproj_math(x_ref[...], g_ref[...], w_ref[...], qg_ref[...], kg_ref[...], hm_ref[...], **dims)
    u_ref[...] = u
    h_ref[...] = h
    ka = (jnp.dot(k.astype(BF16), place_ref[...], preferred_element_type=F32) + kpad_ref[...]).astype(BF16)
    qt, kt, vt = q.T, k.T, v.T
    for hd in range(qt_ref.shape[0]):
        rows = slice(hd * HEAD_DIM, (hd + 1) * HEAD_DIM)
        qt_ref[hd, 0:HEAD_DIM, :] = qt[rows].astype(BF16)
        qt_ref[hd, HEAD_DIM:, :] = qpad_ref[...]
        ka_ref[hd] = ka[:, hd * LANES:(hd + 1) * LANES]
        vt_ref[hd] = vt[rows].astype(BF16)
        kf_ref[hd] = kt[rows]
        vf_ref[hd] = vt[rows]


def _inproj_prompt(x, g, w_bf, qg_t, kg_t, head_mean, place, kpad, qpad, kv_all, *,
                   layer, depth, tm, attn_ch, pool_ch, conv_ch):
    b, t, d = x.shape
    nt = t // tm
    nh = attn_ch // HEAD_DIM
    tok_minor = pl.BlockSpec((None, None, nh, HEAD_DIM, tm), lambda i: (layer, i // nt, 0, 0, i % nt))
    row = lambda c: pl.BlockSpec((None, tm, c), lambda i: (i // nt, i % nt, 0))
    consts = (g, w_bf, qg_t, kg_t, head_mean, place, kpad, qpad)
    carried = () if kv_all is None else tuple(kv_all)
    first_carried = 1 + len(consts)
    kv_shape = jax.ShapeDtypeStruct((depth, b, nh, HEAD_DIM, t), F32)
    return pl.pallas_call(
        functools.partial(_inproj_prompt_kernel, attn_ch=attn_ch, pool_ch=pool_ch, conv_ch=conv_ch),
        grid=(b * nt,),
        in_specs=[row(d)] + [_const_spec(a.shape) for a in consts] + [pl.BlockSpec(memory_space=pl.ANY)] * len(carried),
        out_specs=[
            pl.BlockSpec((None, nh, LANES, tm), lambda i: (i // nt, 0, 0, i % nt)),
            pl.BlockSpec((None, nh, None, tm, LANES), lambda i: (i // nt, 0, i % nt, 0, 0)),
            pl.BlockSpec((None, nh, None, HEAD_DIM, tm), lambda i: (i // nt, 0, i % nt, 0, 0)),
            tok_minor, tok_minor, row(pool_ch), row(conv_ch)],
        out_shape=[
            jax.ShapeDtypeStruct((b, nh, LANES, t), BF16),
            jax.ShapeDtypeStruct((b, nh, nt, tm, LANES), BF16),
            jax.ShapeDtypeStruct((b, nh, nt, HEAD_DIM, tm), BF16),
            kv_shape,
            kv_shape,
            jax.ShapeDtypeStruct((b, t, pool_ch), F32),
            jax.ShapeDtypeStruct((b, t, conv_ch), F32),
        ],
        input_output_aliases={first_carried + n: 3 + n for n in range(len(carried))},
        compiler_params=_params("parallel"),
        name="inproj_prompt",
    )(x, *consts, *carried)


SOFTPLUS2_CLAMP = 100.0


def _softplus2(z2):
    return jnp.maximum(z2, jnp.log(1.0 + jnp.exp2(jnp.minimum(z2, SOFTPLUS2_CLAMP))) * LOG2E)


def _attn_prompt_kernel(qt_ref, k_ref, vt_ref, tri_ref, o_ref, r_ref, za_ref, zb_ref):
    i = pl.program_id(2)
    heads = qt_ref.shape[0]
    tri = tri_ref[...]
    tk, tq = tri.shape[0], qt_ref.shape[-1]

    def scores(j, z_ref):
        for g in range(heads):
            z_ref[g] = jnp.dot(k_ref[g, j], qt_ref[g], preferred_element_type=F32)

    def step(j, zc_ref, zn_ref, diagonal=False):
        scores(jnp.maximum(j - 1, 0), zn_ref)
        if diagonal:
            vis = lax.broadcasted_iota(jnp.int32, (tk, tq), 0) < lax.broadcasted_iota(jnp.int32, (tk, tq), 1)
            sps = [jnp.where(vis, _softplus2(zc_ref[g]), 0.0) for g in range(heads)]
        else:
            sps = [_softplus2(zc_ref[g]) for g in range(heads)]
        ss = [jnp.dot(tri, sp.astype(BF16), preferred_element_type=F32) for sp in sps]
        for g in range(heads):
            w = jnp.exp2(zc_ref[g] - ss[g])
            if diagonal:
                w = jnp.where(vis, w, 0.0)
            pv = jnp.dot(vt_ref[g, j], w.astype(BF16), preferred_element_type=F32)
            if diagonal:
                o_ref[g] = pv
                r_ref[g] = ss[g][0:1, :]
            else:
                r = r_ref[g]
                o_ref[g] += pv * jnp.exp2(-r)
                r_ref[g] = r + ss[g][0:1, :]

    scores(i, za_ref)
    step(i, za_ref, zb_ref, diagonal=True)

    def body(p, carry):
        j = i - 1 - 2 * p
        step(j, zb_ref, za_ref)
        step(j - 1, za_ref, zb_ref)
        return carry

    lax.fori_loop(0, i // 2, body, 0)

    @pl.when(i % 2 == 1)
    def _():
        step(0, zb_ref, za_ref)


def _attn_prompt(qt, k_hm, vt_hm, tri, *, tq, heads):
    b, h, da, t = qt.shape
    nk, tk = k_hm.shape[2], k_hm.shape[3]
    dh = vt_hm.shape[3]
    return pl.pallas_call(
        _attn_prompt_kernel,
        grid=(b, h // heads, t // tq),
        in_specs=[
            pl.BlockSpec((None, heads, da, tq), lambda bi, hi, i: (bi, hi, 0, i)),
            pl.BlockSpec((None, heads, nk, tk, da), lambda bi, hi, i: (bi, hi, 0, 0, 0), pipeline_mode=pl.Buffered(1)),
            pl.BlockSpec((None, heads, nk, dh, tk), lambda bi, hi, i: (bi, hi, 0, 0, 0), pipeline_mode=pl.Buffered(1)),
            _const_spec((tk, tk)),
        ],
        out_specs=pl.BlockSpec((None, heads, dh, tq), lambda bi, hi, i: (bi, hi, 0, i)),
        out_shape=jax.ShapeDtypeStruct((b, h, dh, t), F32),
        scratch_shapes=[pltpu.VMEM((heads, 1, tq), F32)] + [pltpu.VMEM((heads, tk, tq), F32)] * 2,
        compiler_params=_params("parallel", "parallel", "arbitrary"),
        name="attn_prompt",
    )(qt, k_hm, vt_hm, tri)


def _conv_tail(y, lng, lnb, pww, pwb):
    mu = jnp.mean(y, axis=-1, keepdims=True)
    yc = y - mu
    var = jnp.mean(yc * yc, axis=-1, keepdims=True)
    yn = yc * lax.rsqrt(var + EPS) * lng + lnb
    yn = (yn * jax.nn.sigmoid(yn)).astype(BF16)
    return jnp.dot(yn, pww, preferred_element_type=F32) + pwb


def _pool_window_of_lane(shape):
    lane = lax.broadcasted_iota(jnp.int32, shape, len(shape) - 1)
    gw = shape[-1] // len(POOL_WINDOWS)
    win = jnp.full(shape, POOL_WINDOWS[-1], jnp.int32)
    for g in reversed(range(len(POOL_WINDOWS) - 1)):
        win = jnp.where(lane < (g + 1) * gw, POOL_WINDOWS[g], win)
    return win


HALO = 32


def _mix_prompt_kernel(u_ref, up_ref, h_ref, hp_ref, poolw_ref, pscale_ref, dww_ref, dwb_ref,
                       lng_ref, lnb_ref, pww_ref, pwb_ref, pool_o, conv_o, uext, hext, ush, hsh):
    i = pl.program_id(1)
    tt, c = u_ref.shape

    @pl.when(i == 0)
    def _():
        uext[0:HALO, :] = jnp.zeros((HALO, c), F32)
        hext[0:HALO, :] = jnp.zeros((HALO, c), F32)

    @pl.when(i > 0)
    def _():
        uext[0:HALO, :] = up_ref[...]
        hext[0:HALO, :] = hp_ref[...]

    u = u_ref[...]
    uext[HALO:, :] = u
    hext[HALO:, :] = h_ref[...]

    for off in range(1, SUBLANES):
        ush[off - 1] = uext[pl.ds(off, HALO + tt - SUBLANES), :]
        hsh[off - 1] = hext[pl.ds(off, HALO + tt - SUBLANES), :]

    def window(ext, shifted, r0):
        a, off = divmod(r0, SUBLANES)
        if off == 0:
            return ext[pl.ds(r0, tt), :]
        return shifted[off - 1, pl.ds(a * SUBLANES, tt), :]

    sums = []
    run = u
    k = 1
    for wnd in POOL_WINDOWS:
        while k < wnd:
            run = run + window(uext, ush, HALO - k)
            k += 1
        sums.append(run)
    win = _pool_window_of_lane((tt, c))
    ssel = sums[-1]
    for g in reversed(range(len(POOL_WINDOWS) - 1)):
        ssel = jnp.where(win == POOL_WINDOWS[g], sums[g], ssel)
    pos = i * tt + lax.broadcasted_iota(jnp.int32, (tt, c), 0)
    cnt = jnp.minimum(win, pos + 1).astype(F32)
    d = (ssel / cnt - u).astype(BF16)
    pool_o[...] = (jnp.dot(d, poolw_ref[...], preferred_element_type=F32) * pscale_ref[...]).astype(BF16)

    y = jnp.zeros((tt, c), F32) + dwb_ref[...]
    for tap in range(CONV_TAPS):
        y = y + window(hext, hsh, HALO - CONV_HIST + tap) * dww_ref[tap:tap + 1, :]
    conv_o[...] = _conv_tail(y, lng_ref[...], lnb_ref[...], pww_ref[...], pwb_ref[...]).astype(BF16)


def _mix_prompt(u, h, poolw_bd, pscale, dww, dwb, lng, lnb, pww, pwb, *, tt):
    b, t, c = u.shape
    r = tt // HALO
    cur = pl.BlockSpec((None, tt, c), lambda bi, i: (bi, i, 0))
    prev = pl.BlockSpec((None, HALO, c), lambda bi, i: (bi, jnp.maximum(i * r - 1, 0), 0))
    vec = _const_spec((1, c))
    return pl.pallas_call(
        _mix_prompt_kernel,
        grid=(b, t // tt),
        in_specs=[cur, prev, cur, prev, _const_spec((c, c)), vec, _const_spec(dww.shape), vec,
                  vec, vec, _const_spec((c, c)), vec],
        out_specs=[cur, cur],
        out_shape=[jax.ShapeDtypeStruct((b, t, c), BF16)] * 2,
        scratch_shapes=[pltpu.VMEM((HALO + tt, c), F32)] * 2
        + [pltpu.VMEM((SUBLANES - 1, HALO + tt - SUBLANES, c), F32)] * 2,
        compiler_params=_params("parallel", "arbitrary"),
        name="mix_prompt",
    )(u, u, h, h, poolw_bd, pscale, dww, dwb, lng, lnb, pww, pwb)


def _mix_decode_kernel(u_ref, sp_ref, h_ref, sc_ref, poolw_ref, pscale_ref, dww_ref, dwb_ref,
                       lng_ref, lnb_ref, pww_ref, pwb_ref, pool_o, conv_o, *, start_pos):
    u = u_ref[...]
    sums = []
    run = u
    k = 1
    for wnd in POOL_WINDOWS:
        while k < wnd:
            run = run + sp_ref[POOL_HIST - k]
            k += 1
        sums.append(run)
    win = _pool_window_of_lane(u.shape)
    ssel = sums[-1]
    for g in reversed(range(len(POOL_WINDOWS) - 1)):
        ssel = jnp.where(win == POOL_WINDOWS[g], sums[g], ssel)
    cnt = jnp.minimum(win, start_pos + 1).astype(F32)
    d = (ssel / cnt - u).astype(BF16)
    pool_o[...] = (jnp.dot(d, poolw_ref[...], preferred_element_type=F32) * pscale_ref[...]).astype(BF16)

    y = h_ref[...] * dww_ref[CONV_HIST:CONV_TAPS, :] + dwb_ref[...]
    for tap in range(CONV_HIST):
        y = y + sc_ref[tap] * dww_ref[tap:tap + 1, :]
    conv_o[...] = _conv_tail(y, lng_ref[...], lnb_ref[...], pww_ref[...], pwb_ref[...]).astype(BF16)


def _mix_decode(u, state_pool_tm, h, state_conv_tm, poolw_bd, pscale, dww, dwb, lng, lnb, pww, pwb, *, start_pos):
    db, c = u.shape
    args = (u, state_pool_tm, h, state_conv_tm, poolw_bd, pscale, dww, dwb, lng, lnb, pww, pwb)
    return pl.pallas_call(
        functools.partial(_mix_decode_kernel, start_pos=start_pos),
        grid=(1,),
        in_specs=[_const_spec(a.shape) for a in args],
        out_specs=[_const_spec((db, c))] * 2,
        out_shape=[jax.ShapeDtypeStruct((db, c), BF16)] * 2,
        compiler_params=_params("arbitrary"),
        name="mix_decode",
    )(*args)


def _merge_ffn_kernel(x_ref, att_ref, pool_ref, conv_ref, wo_ref, g_ref, wup_ref, wdn_ref, o_ref, *, ff_chunk):
    cp = pool_ref.shape[1]
    if len(att_ref.shape) == 3:
        nh, dh, rows = att_ref.shape
        att = att_ref[...].reshape(nh * dh, rows).T.astype(BF16)
    else:
        att = att_ref[...]
    ca = att.shape[1]
    m = jnp.dot(att, wo_ref[0:ca, :], preferred_element_type=F32)
    m = m + jnp.dot(pool_ref[...], wo_ref[ca:ca + cp, :], preferred_element_type=F32)
    m = m + jnp.dot(conv_ref[...], wo_ref[ca + cp:, :], preferred_element_type=F32)
    x1 = x_ref[...] + m
    ms = jnp.mean(x1 * x1, axis=-1, keepdims=True)
    xn = (x1 * lax.rsqrt(ms + EPS) * g_ref[...]).astype(BF16)
    out = x1
    d_ff = wup_ref.shape[1]
    for c0 in range(0, d_ff, ff_chunk):
        hmid = jnp.dot(xn, wup_ref[:, c0:c0 + ff_chunk], preferred_element_type=F32)
        hmid = jnp.square(jnp.maximum(hmid, 0.0)).astype(BF16)
        out = out + jnp.dot(hmid, wdn_ref[c0:c0 + ff_chunk, :], preferred_element_type=F32)
    o_ref[...] = out


def _merge_ffn(x, att, pool, conv, wo_bf, g, wup_bf, wdn_bf, *, tm, ff_chunk):
    m, d = x.shape
    row = lambda c: pl.BlockSpec((tm, c), lambda i: (i, 0))
    single = lambda shape: pl.BlockSpec(shape, lambda i: (0, 0), pipeline_mode=pl.Buffered(1))
    if att.ndim == 4:
        nt = att.shape[3] // tm
        att_spec = pl.BlockSpec((None,) + att.shape[1:3] + (tm,), lambda i: (i // nt, 0, 0, i % nt))
    else:
        att_spec = row(att.shape[1])
    return pl.pallas_call(
        functools.partial(_merge_ffn_kernel, ff_chunk=ff_chunk),
        grid=(m // tm,),
        in_specs=[row(d), att_spec, row(pool.shape[1]), row(conv.shape[1]),
                  single(wo_bf.shape), single((1, d)), single(wup_bf.shape), single(wdn_bf.shape)],
        out_specs=row(d),
        out_shape=jax.ShapeDtypeStruct((m, d), F32),
        compiler_params=_params("parallel"),
        name="merge_ffn",
    )(x, att, pool, conv, wo_bf, g, wup_bf, wdn_bf)


def _ffn_decode_kernel(pt_ref, x_ref, att_ref, pool_ref, conv_ref, wo_ref, g_ref, wup_ref, wdn_ref,
                       q_ref, bias_ref, tri_ref, *refs, pages, groups, ff_chunk):
    del pt_ref
    k_refs, v_refs = refs[:pages], refs[pages:2 * pages]
    o_ref, od_ref, acc_ref, r_ref = refs[2 * pages:]
    j = pl.program_id(0) % groups

    @pl.when(j == 0)
    def _():
        acc_ref[...] = jnp.zeros_like(acc_ref)
        r_ref[...] = jnp.zeros_like(r_ref)

    q = q_ref[...]
    bias = bias_ref[...]
    tri = tri_ref[...]

    def sweep(page_ids, r):
        for p in page_ids:
            z = jnp.sum(k_refs[p][...] * q, axis=1) + bias
            s = jnp.dot(_softplus2(z).astype(BF16), tri, preferred_element_type=F32)
            w = jnp.exp2(z - s - r)
            acc_ref[...] += w[:, None, :] * v_refs[p][...]
            r = r + s[:, 0:1]
        return r

    nh, dh, rows = att_ref.shape
    att = att_ref[...].reshape(nh * dh, rows).T.astype(BF16)
    ca, cp = att.shape[1], pool_ref.shape[1]
    m = jnp.dot(att, wo_ref[0:ca, :], preferred_element_type=F32)
    m = m + jnp.dot(pool_ref[...], wo_ref[ca:ca + cp, :], preferred_element_type=F32)
    m = m + jnp.dot(conv_ref[...], wo_ref[ca + cp:, :], preferred_element_type=F32)
    x1 = x_ref[...] + m
    ms = jnp.mean(x1 * x1, axis=-1, keepdims=True)
    xn = (x1 * lax.rsqrt(ms + EPS) * g_ref[...]).astype(BF16)
    out = x1
    d_ff = wup_ref.shape[1]
    n_chunks = d_ff // ff_chunk
    per = pages // n_chunks
    order = list(reversed(range(pages)))
    r = r_ref[...]
    for c in range(n_chunks):
        c0 = c * ff_chunk
        hmid = jnp.dot(xn, wup_ref[:, c0:c0 + ff_chunk], preferred_element_type=F32)
        hmid = jnp.square(jnp.maximum(hmid, 0.0)).astype(BF16)
        out = out + jnp.dot(hmid, wdn_ref[c0:c0 + ff_chunk, :], preferred_element_type=F32)
        r = sweep(order[c * per:(c + 1) * per], r)
    o_ref[...] = out
    r_ref[...] = r

    @pl.when(j == groups - 1)
    def _():
        od_ref[...] = jnp.sum(acc_ref[...], axis=-1)


def _ffn_decode(x, att, pool, conv, wo_bf, g, wup_bf, wdn_bf,
                page_table, q_rep, bias_rep, tri, cache_kt, cache_vt, *, layer, tm, ff_chunk):
    m, d = x.shape
    steps = m // tm
    db, n_pages = page_table.shape
    nh, dh, page = cache_kt.shape[2:]
    groups = steps // db
    pages = n_pages // groups
    assert steps == db * groups and n_pages == pages * groups
    nt = att.shape[3] // tm

    row = lambda c: pl.BlockSpec((tm, c), lambda i, pt: (i, 0))
    single = lambda shape: pl.BlockSpec(shape, lambda i, pt: (0,) * len(shape), pipeline_mode=pl.Buffered(1))

    def page_spec(p):
        return pl.BlockSpec(
            (None, None, nh, dh, page),
            lambda i, pt: (layer, pt[i // groups, (groups - 1 - i % groups) * pages + p], 0, 0, 0))

    grid_spec = pltpu.PrefetchScalarGridSpec(
        num_scalar_prefetch=1,
        grid=(steps,),
        in_specs=[
            row(d),
            pl.BlockSpec((None,) + att.shape[1:3] + (tm,), lambda i, pt: (i // nt, 0, 0, i % nt)),
            row(pool.shape[1]), row(conv.shape[1]),
            single(wo_bf.shape), single((1, d)), single(wup_bf.shape), single(wdn_bf.shape),
            pl.BlockSpec((None, nh, dh, page), lambda i, pt: (i // groups, 0, 0, 0)),
            single((nh, page)), single((page, page)),
        ] + [page_spec(p) for p in range(pages)] * 2,
        out_specs=[row(d), pl.BlockSpec((None, nh, dh), lambda i, pt: (i // groups, 0, 0))],
        scratch_shapes=[pltpu.VMEM((nh, dh, page), F32), pltpu.VMEM((nh, 1), F32)],
    )
    return pl.pallas_call(
        functools.partial(_ffn_decode_kernel, pages=pages, groups=groups, ff_chunk=ff_chunk),
        grid_spec=grid_spec,
        out_shape=[jax.ShapeDtypeStruct((m, d), F32), jax.ShapeDtypeStruct((db, nh, dh), F32)],
        compiler_params=_params("arbitrary"),
        name="ffn_decode",
    )(page_table, x, att, pool, conv, wo_bf, g, wup_bf, wdn_bf, q_rep, bias_rep, tri,
      *([cache_kt] * pages), *([cache_vt] * pages))


def _block_diag(blocks):
    g, a, b = blocks.shape
    eye = jnp.eye(g, dtype=blocks.dtype)
    return (eye[:, None, :, None] * blocks[:, :, None, :]).reshape(g * a, g * b)


def _reverse_tri(n):
    return (jnp.arange(n)[None, :] >= jnp.arange(n)[:, None]).astype(BF16)


ATTN_TQ = 256
ATTN_HEADS_PER_STEP = 8
PROMPT_TM = 256
MIX_TT = 512
FF_CHUNK = 1024
FFN_DECODE_TM = 128


def kernel(x_prompt, x_sample, cache_k, cache_v, state_pool, state_conv, page_table, norm_mix_g, w_in, q_norm_g, k_norm_g, sb_bias, pool_w, pool_scale, conv_dw_w, conv_dw_b, conv_ln_g, conv_ln_b, conv_pw_w, conv_pw_b, w_out, norm_ffn_g, w_ffn_up, w_ffn_down):
    b, t, d = x_prompt.shape
    db, ts, _ = x_sample.shape
    assert ts == 1
    assert PROMPT_TM == ATTN_TQ
    depth = w_in.shape[0]
    n_heads = sb_bias.shape[1]
    attn_ch = n_heads * HEAD_DIM
    pool_ch = pool_scale.shape[1]
    conv_ch = conv_dw_b.shape[1]
    past_len = page_table.shape[1] * PAGE_SIZE

    xp = x_prompt.reshape(b * t, d)
    xs = x_sample.reshape(db, d)

    head_mean = _block_diag(jnp.full((n_heads, HEAD_DIM, HEAD_DIM), 1.0 / HEAD_DIM, BF16))
    tri_prompt = _reverse_tri(ATTN_TQ)
    tri_page = _reverse_tri(PAGE_SIZE).T
    cache_kt = cache_k.transpose(0, 1, 3, 4, 2)
    cache_vt = cache_v.transpose(0, 1, 3, 4, 2)
    ch = jnp.arange(attn_ch)
    place = (jnp.arange(n_heads * LANES)[None, :] == ((ch // HEAD_DIM) * LANES + ch % HEAD_DIM)[:, None]).astype(BF16)
    qpad = jnp.zeros((LANES - HEAD_DIM, PROMPT_TM), BF16).at[0:2, :].set(1.0)
    row2 = lambda v: v.reshape(1, -1)

    outs = {n: [] for n in ("ks", "vs", "pp", "ps", "cp", "cs")}
    kv_all = None
    for l in range(depth):
        poolw_bd = _block_diag(pool_w[l]).astype(BF16)
        mix_w = (poolw_bd, row2(pool_scale[l]), conv_dw_w[l], row2(conv_dw_b[l]),
                 row2(conv_ln_g[l]), row2(conv_ln_b[l]), conv_pw_w[l].astype(BF16), row2(conv_pw_b[l]))
        in_w = (row2(norm_mix_g[l]), w_in[l].astype(BF16), row2(jnp.tile(q_norm_g[l], n_heads)),
                row2(jnp.tile(k_norm_g[l], n_heads)), head_mean)
        dims = dict(attn_ch=attn_ch, pool_ch=pool_ch, conv_ch=conv_ch)
        ffn_w = (w_out[l].astype(BF16), row2(norm_ffn_g[l]), w_ffn_up[l].astype(BF16), w_ffn_down[l].astype(BF16))
        bias2 = sb_bias[l] * LOG2E
        bias_hi = bias2.astype(BF16)
        bias_lo = (bias2 - bias_hi.astype(F32)).astype(BF16)
        kpad = jnp.zeros((n_heads, LANES), F32).at[:, HEAD_DIM].set(bias_hi.astype(F32))
        kpad = kpad.at[:, HEAD_DIM + 1].set(bias_lo.astype(F32)).reshape(1, n_heads * LANES)

        qt, k_aug, vt, kt_all, vt_all, up, hp = _inproj_prompt(
            x_prompt if l == 0 else xp.reshape(b, t, d), *in_w, place, kpad, qpad, kv_all,
            layer=l, depth=depth, tm=PROMPT_TM, **dims)
        kv_all = (kt_all, vt_all)
        qs, ksf, vsf, us, hs = _inproj_decode(xs, *in_w, **dims)

        att_t = _attn_prompt(qt, k_aug, vt, tri_prompt, tq=ATTN_TQ, heads=ATTN_HEADS_PER_STEP)

        q_rep = jnp.broadcast_to(qs.reshape(db, n_heads, HEAD_DIM, 1), (db, n_heads, HEAD_DIM, PAGE_SIZE))
        bias_rep = jnp.broadcast_to(bias2[:, None], (n_heads, PAGE_SIZE))

        pool_p, conv_p = _mix_prompt(up, hp, *mix_w, tt=MIX_TT)
        pool_s, conv_s = _mix_decode(us, state_pool[l].transpose(1, 0, 2), hs, state_conv[l].transpose(1, 0, 2),
                                     *mix_w, start_pos=past_len)

        xp, att_s = _ffn_decode(xp, att_t, pool_p.reshape(b * t, pool_ch), conv_p.reshape(b * t, conv_ch), *ffn_w,
                                page_table, q_rep, bias_rep, tri_page, cache_kt, cache_vt,
                                layer=l, tm=FFN_DECODE_TM, ff_chunk=FF_CHUNK)
        xs = _merge_ffn(xs, att_s.reshape(db, attn_ch).astype(BF16), pool_s, conv_s, *ffn_w, tm=db, ff_chunk=FF_CHUNK)

        outs["ks"].append(ksf.reshape(db, ts, n_heads, HEAD_DIM))
        outs["vs"].append(vsf.reshape(db, ts, n_heads, HEAD_DIM))
        outs["pp"].append(up[:, t - POOL_HIST:])
        outs["ps"].append(jnp.concatenate([state_pool[l][:, 1:], us[:, None, :]], axis=1))
        outs["cp"].append(hp[:, t - CONV_HIST:])
        outs["cs"].append(jnp.concatenate([state_conv[l][:, 1:], hs[:, None, :]], axis=1))

    return (xp.reshape(b, t, d), xs.reshape(db, ts, d),
            kv_all[0].transpose(0, 1, 4, 2, 3), kv_all[1].transpose(0, 1, 4, 2, 3),
            jnp.stack(outs["ks"]), jnp.stack(outs["vs"]),
            jnp.stack(outs["pp"]), jnp.stack(outs["ps"]), jnp.stack(outs["cp"]), jnp.stack(outs["cs"]))
```

```python
import functools

import jax
import jax.numpy as jnp
from jax import lax
from jax.experimental import pallas as pl
from jax.experimental.pallas import tpu as pltpu

F32 = jnp.float32
BF16 = jnp.bfloat16

EPS = 1e-6
LOG2E = 1.4426950408889634
HEAD_DIM = 64
POOL_WINDOWS = (2, 4, 8, 16)
POOL_HIST = max(POOL_WINDOWS) - 1
CONV_TAPS = 31
CONV_HIST = CONV_TAPS - 1
PAGE_SIZE = 128

V7X_VMEM_LIMIT_BYTES = 56 * 1024 * 1024
LANES = 128
SUBLANES = 8


def _params(*sem):
    return pltpu.CompilerParams(dimension_semantics=sem, vmem_limit_bytes=V7X_VMEM_LIMIT_BYTES)


def _const_spec(shape):
    nd = len(shape)
    return pl.BlockSpec(shape, lambda *_: (0,) * nd)


def _inproj_math(x, g, w, qg, kg, hm, *, attn_ch, pool_ch, conv_ch):
    ms = jnp.mean(x * x, axis=-1, keepdims=True)
    xn = (x * lax.rsqrt(ms + EPS) * g).astype(BF16)
    z = jnp.dot(xn, w, preferred_element_type=F32)
    c0, c1, c2 = attn_ch, 2 * attn_ch, 3 * attn_ch
    c3 = c2 + pool_ch
    c4 = c3 + conv_ch
    q, k, v = z[:, :c0], z[:, c0:c1], z[:, c1:c2]
    u, a, gate = z[:, c2:c3], z[:, c3:c4], z[:, c4:]

    def head_norm(t, gain):
        msq = jnp.dot((t * t).astype(BF16), hm, preferred_element_type=F32)
        return t * lax.rsqrt(msq + EPS) * gain

    q_scaled = head_norm(q, qg) * (HEAD_DIM ** -0.5 * LOG2E)
    return q_scaled, head_norm(k, kg), v, u, a * jax.nn.sigmoid(gate)


def _inproj_decode_kernel(x_ref, g_ref, w_ref, qg_ref, kg_ref, hm_ref,
                          q_ref, k_ref, v_ref, u_ref, h_ref, **dims):
    q, k, v, u, h = _inproj_math(x_ref[...], g_ref[...], w_ref[...], qg_ref[...], kg_ref[...], hm_ref[...], **dims)
    q_ref[...] = q
    k_ref[...] = k
    v_ref[...] = v
    u_ref[...] = u
    h_ref[...] = h


def _inproj_decode(x, g, w_bf, qg_t, kg_t, head_mean, *, attn_ch, pool_ch, conv_ch):
    m, d = x.shape
    widths = (attn_ch, attn_ch, attn_ch, pool_ch, conv_ch)
    args = (x, g, w_bf, qg_t, kg_t, head_mean)
    return pl.pallas_call(
        functools.partial(_inproj_decode_kernel, attn_ch=attn_ch, pool_ch=pool_ch, conv_ch=conv_ch),
        grid=(1,),
        in_specs=[_const_spec(a.shape) for a in args],
        out_specs=[_const_spec((m, c)) for c in widths],
        out_shape=[jax.ShapeDtypeStruct((m, c), F32) for c in widths],
        compiler_params=_params("arbitrary"),
        name="inproj_decode",
    )(*args)


def _inproj_prompt_kernel(x_ref, g_ref, w_ref, qg_ref, kg_ref, hm_ref, place_ref, kpad_ref, qpad_ref, *rest, **dims):
    qt_ref, ka_ref, vt_ref, kf_ref, vf_ref, u_ref, h_ref = rest[-7:]
    q, k, v, u, h = _inproj_math(x_ref[...], g_ref[...], w_ref[...], qg_ref[...], kg_ref[...], hm_ref[...], **dims)
    u_ref[...] = u
    h_ref[...] = h
    ka = (jnp.dot(k.astype(BF16), place_ref[...], preferred_element_type=F32) + kpad_ref[...]).astype(BF16)
    qt, kt, vt = q.T, k.T, v.T
    for hd in range(qt_ref.shape[0]):
        rows = slice(hd * HEAD_DIM, (hd + 1) * HEAD_DIM)
        qt_ref[hd, 0:HEAD_DIM, :] = qt[rows].astype(BF16)
        qt_ref[hd, HEAD_DIM:, :] = qpad_ref[...]
        ka_ref[hd] = ka[:, hd * LANES:(hd + 1) * LANES]
        vt_ref[hd] = vt[rows].astype(BF16)
        kf_ref[hd] = kt[rows]
        vf_ref[hd] = vt[rows]


def _inproj_prompt(x, g, w_bf, qg_t, kg_t, head_mean, place, kpad, qpad, kv_all, *,
                   layer, depth, tm, attn_ch, pool_ch, conv_ch):
    b, t, d = x.shape
    nt = t // tm
    nh = attn_ch // HEAD_DIM
    tok_minor = pl.BlockSpec((None, None, nh, HEAD_DIM, tm), lambda i: (layer, i // nt, 0, 0, i % nt))
    row = lambda c: pl.BlockSpec((None, tm, c), lambda i: (i // nt, i % nt, 0))
    consts = (g, w_bf, qg_t, kg_t, head_mean, place, kpad, qpad)
    carried = () if kv_all is None else tuple(kv_all)
    first_carried = 1 + len(consts)
    kv_shape = jax.ShapeDtypeStruct((depth, b, nh, HEAD_DIM, t), F32)
    return pl.pallas_call(
        functools.partial(_inproj_prompt_kernel, attn_ch=attn_ch, pool_ch=pool_ch, conv_ch=conv_ch),
        grid=(b * nt,),
        in_specs=[row(d)] + [_const_spec(a.shape) for a in consts] + [pl.BlockSpec(memory_space=pl.ANY)] * len(carried),
        out_specs=[
            pl.BlockSpec((None, nh, LANES, tm), lambda i: (i // nt, 0, 0, i % nt)),
            pl.BlockSpec((None, nh, None, tm, LANES), lambda i: (i // nt, 0, i % nt, 0, 0)),
            pl.BlockSpec((None, nh, None, HEAD_DIM, tm), lambda i: (i // nt, 0, i % nt, 0, 0)),
            tok_minor, tok_minor, row(pool_ch), row(conv_ch)],
        out_shape=[
            jax.ShapeDtypeStruct((b, nh, LANES, t), BF16),
            jax.ShapeDtypeStruct((b, nh, nt, tm, LANES), BF16),
            jax.ShapeDtypeStruct((b, nh, nt, HEAD_DIM, tm), BF16),
            kv_shape,
            kv_shape,
            jax.ShapeDtypeStruct((b, t, pool_ch), F32),
            jax.ShapeDtypeStruct((b, t, conv_ch), F32),
        ],
        input_output_aliases={first_carried + n: 3 + n for n in range(len(carried))},
        compiler_params=_params("parallel"),
        name="inproj_prompt",
    )(x, *consts, *carried)


SOFTPLUS2_CLAMP = 100.0


def _softplus2(z2):
    return jnp.maximum(z2, jnp.log(1.0 + jnp.exp2(jnp.minimum(z2, SOFTPLUS2_CLAMP))) * LOG2E)


def _attn_prompt_kernel(qt_ref, k_ref, vt_ref, tri_ref, o_ref, r_ref, za_ref, zb_ref):
    i = pl.program_id(2)
    heads = qt_ref.shape[0]
    tri = tri_ref[...]
    tk, tq = tri.shape[0], qt_ref.shape[-1]

    def scores(j, z_ref):
        for g in range(heads):
            z_ref[g] = jnp.dot(k_ref[g, j], qt_ref[g], preferred_element_type=F32)

    def step(j, zc_ref, zn_ref, diagonal=False):
        scores(jnp.maximum(j - 1, 0), zn_ref)
        if diagonal:
            vis = lax.broadcasted_iota(jnp.int32, (tk, tq), 0) < lax.broadcasted_iota(jnp.int32, (tk, tq), 1)
            sps = [jnp.where(vis, _softplus2(zc_ref[g]), 0.0) for g in range(heads)]
        else:
            sps = [_softplus2(zc_ref[g]) for g in range(heads)]
        ss = [jnp.dot(tri, sp.astype(BF16), preferred_element_type=F32) for sp in sps]
        for g in range(heads):
            w = jnp.exp2(zc_ref[g] - ss[g])
            if diagonal:
                w = jnp.where(vis, w, 0.0)
            pv = jnp.dot(vt_ref[g, j], w.astype(BF16), preferred_element_type=F32)
            if diagonal:
                o_ref[g] = pv
                r_ref[g] = ss[g][0:1, :]
            else:
                r = r_ref[g]
                o_ref[g] += pv * jnp.exp2(-r)
                r_ref[g] = r + ss[g][0:1, :]

    scores(i, za_ref)
    step(i, za_ref, zb_ref, diagonal=True)

    def body(p, carry):
        j = i - 1 - 2 * p
        step(j, zb_ref, za_ref)
        step(j - 1, za_ref, zb_ref)
        return carry

    lax.fori_loop(0, i // 2, body, 0)

    @pl.when(i % 2 == 1)
    def _():
        step(0, zb_ref, za_ref)


def _attn_prompt(qt, k_hm, vt_hm, tri, *, tq, heads):
    b, h, da, t = qt.shape
    nk, tk = k_hm.shape[2], k_hm.shape[3]
    dh = vt_hm.shape[3]
    return pl.pallas_call(
        _attn_prompt_kernel,
        grid=(b, h // heads, t // tq),
        in_specs=[
            pl.BlockSpec((None, heads, da, tq), lambda bi, hi, i: (bi, hi, 0, i)),
            pl.BlockSpec((None, heads, nk, tk, da), lambda bi, hi, i: (bi, hi, 0, 0, 0), pipeline_mode=pl.Buffered(1)),
            pl.BlockSpec((None, heads, nk, dh, tk), lambda bi, hi, i: (bi, hi, 0, 0, 0), pipeline_mode=pl.Buffered(1)),
            _const_spec((tk, tk)),
        ],
        out_specs=pl.BlockSpec((None, heads, dh, tq), lambda bi, hi, i: (bi, hi, 0, i)),
        out_shape=jax.ShapeDtypeStruct((b, h, dh, t), F32),
        scratch_shapes=[pltpu.VMEM((heads, 1, tq), F32)] + [pltpu.VMEM((heads, tk, tq), F32)] * 2,
        compiler_params=_params("parallel", "parallel", "arbitrary"),
        name="attn_prompt",
    )(qt, k_hm, vt_hm, tri)


def _attn_decode_kernel(pt_ref, q_ref, bias_ref, tri_ref, *refs, pages):
    del pt_ref
    k_refs, v_refs = refs[:pages], refs[pages:2 * pages]
    o_ref, acc_ref, r_ref = refs[2 * pages:]
    j = pl.program_id(1)

    @pl.when(j == 0)
    def _():
        acc_ref[...] = jnp.zeros_like(acc_ref)
        r_ref[...] = jnp.zeros_like(r_ref)

    bias = bias_ref[...]
    tri = tri_ref[...]
    r = r_ref[...]
    nh = q_ref.shape[0]
    order = list(reversed(range(pages)))
    for g0 in range(0, pages, DECODE_PAGE_GROUP):
        group = order[g0:g0 + DECODE_PAGE_GROUP]
        zrows = [[] for _ in group]
        for h in range(nh):
            qh = q_ref[h]
            for n, p in enumerate(group):
                zrows[n].append(jnp.sum(k_refs[p][h] * qh, axis=0, keepdims=True))
        ws = []
        for n in range(len(group)):
            z = jnp.concatenate(zrows[n], axis=0) + bias
            s = jnp.dot(_softplus2(z).astype(BF16), tri, preferred_element_type=F32)
            ws.append(jnp.exp2(z - s - r))
            r = r + s[:, 0:1]
        for h in range(nh):
            acc = acc_ref[h]
            for n, p in enumerate(group):
                acc = acc + ws[n][h:h + 1, :] * v_refs[p][h]
            acc_ref[h] = acc
    r_ref[...] = r

    @pl.when(j == pl.num_programs(1) - 1)
    def _():
        o_ref[...] = jnp.sum(acc_ref[...], axis=-1)


def _attn_decode(page_table, q_rep, bias_rep, tri, cache_kt, cache_vt, *, layer, pages):
    db, n_pages = page_table.shape
    nh, dh, page = cache_kt.shape[2:]
    groups = n_pages // pages

    def page_spec(p):
        return pl.BlockSpec(
            (None, None, nh, dh, page),
            lambda b, j, pt: (layer, pt[b, (groups - 1 - j) * pages + p], 0, 0, 0))

    grid_spec = pltpu.PrefetchScalarGridSpec(
        num_scalar_prefetch=1,
        grid=(db, groups),
        in_specs=[
            pl.BlockSpec((None, nh, dh, page), lambda b, j, pt: (b, 0, 0, 0)),
            pl.BlockSpec((nh, page), lambda b, j, pt: (0, 0)),
            pl.BlockSpec((page, page), lambda b, j, pt: (0, 0)),
        ] + [page_spec(p) for p in range(pages)] * 2,
        out_specs=pl.BlockSpec((None, nh, dh), lambda b, j, pt: (b, 0, 0)),
        scratch_shapes=[pltpu.VMEM((nh, dh, page), F32), pltpu.VMEM((nh, 1), F32)],
    )
    return pl.pallas_call(
        functools.partial(_attn_decode_kernel, pages=pages),
        grid_spec=grid_spec,
        out_shape=jax.ShapeDtypeStruct((db, nh, dh), F32),
        compiler_params=_params("parallel", "arbitrary"),
        name="attn_decode",
    )(page_table, q_rep, bias_rep, tri, *([cache_kt] * pages), *([cache_vt] * pages))


def _conv_tail(y, lng, lnb, pww, pwb):
    mu = jnp.mean(y, axis=-1, keepdims=True)
    yc = y - mu
    var = jnp.mean(yc * yc, axis=-1, keepdims=True)
    yn = yc * lax.rsqrt(var + EPS) * lng + lnb
    yn = (yn * jax.nn.sigmoid(yn)).astype(BF16)
    return jnp.dot(yn, pww, preferred_element_type=F32) + pwb


def _pool_window_of_lane(shape):
    lane = lax.broadcasted_iota(jnp.int32, shape, len(shape) - 1)
    gw = shape[-1] // len(POOL_WINDOWS)
    win = jnp.full(shape, POOL_WINDOWS[-1], jnp.int32)
    for g in reversed(range(len(POOL_WINDOWS) - 1)):
        win = jnp.where(lane < (g + 1) * gw, POOL_WINDOWS[g], win)
    return win


HALO = 32


def _mix_prompt_kernel(u_ref, up_ref, h_ref, hp_ref, poolw_ref, pscale_ref, dww_ref, dwb_ref,
                       lng_ref, lnb_ref, pww_ref, pwb_ref, pool_o, conv_o, uext, hext, ush, hsh):
    i = pl.program_id(1)
    tt, c = u_ref.shape

    @pl.when(i == 0)
    def _():
        uext[0:HALO, :] = jnp.zeros((HALO, c), F32)
        hext[0:HALO, :] = jnp.zeros((HALO, c), F32)

    @pl.when(i > 0)
    def _():
        uext[0:HALO, :] = up_ref[...]
        hext[0:HALO, :] = hp_ref[...]

    u = u_ref[...]
    uext[HALO:, :] = u
    hext[HALO:, :] = h_ref[...]

    for off in range(1, SUBLANES):
        ush[off - 1] = uext[pl.ds(off, HALO + tt - SUBLANES), :]
        hsh[off - 1] = hext[pl.ds(off, HALO + tt - SUBLANES), :]

    def window(ext, shifted, r0):
        a, off = divmod(r0, SUBLANES)
        if off == 0:
            return ext[pl.ds(r0, tt), :]
        return shifted[off - 1, pl.ds(a * SUBLANES, tt), :]

    sums = []
    run = u
    k = 1
    for wnd in POOL_WINDOWS:
        while k < wnd:
            run = run + window(uext, ush, HALO - k)
            k += 1
        sums.append(run)
    win = _pool_window_of_lane((tt, c))
    ssel = sums[-1]
    for g in reversed(range(len(POOL_WINDOWS) - 1)):
        ssel = jnp.where(win == POOL_WINDOWS[g], sums[g], ssel)
    pos = i * tt + lax.broadcasted_iota(jnp.int32, (tt, c), 0)
    cnt = jnp.minimum(win, pos + 1).astype(F32)
    d = (ssel / cnt - u).astype(BF16)
    pool_o[...] = (jnp.dot(d, poolw_ref[...], preferred_element_type=F32) * pscale_ref[...]).astype(BF16)

    y = jnp.zeros((tt, c), F32) + dwb_ref[...]
    for tap in range(CONV_TAPS):
        y = y + window(hext, hsh, HALO - CONV_HIST + tap) * dww_ref[tap:tap + 1, :]
    conv_o[...] = _conv_tail(y, lng_ref[...], lnb_ref[...], pww_ref[...], pwb_ref[...]).astype(BF16)


def _mix_prompt(u, h, poolw_bd, pscale, dww, dwb, lng, lnb, pww, pwb, *, tt):
    b, t, c = u.shape
    r = tt // HALO
    cur = pl.BlockSpec((None, tt, c), lambda bi, i: (bi, i, 0))
    prev = pl.BlockSpec((None, HALO, c), lambda bi, i: (bi, jnp.maximum(i * r - 1, 0), 0))
    vec = _const_spec((1, c))
    return pl.pallas_call(
        _mix_prompt_kernel,
        grid=(b, t // tt),
        in_specs=[cur, prev, cur, prev, _const_spec((c, c)), vec, _const_spec(dww.shape), vec,
                  vec, vec, _const_spec((c, c)), vec],
        out_specs=[cur, cur],
        out_shape=[jax.ShapeDtypeStruct((b, t, c), BF16)] * 2,
        scratch_shapes=[pltpu.VMEM((HALO + tt, c), F32)] * 2
        + [pltpu.VMEM((SUBLANES - 1, HALO + tt - SUBLANES, c), F32)] * 2,
        compiler_params=_params("parallel", "arbitrary"),
        name="mix_prompt",
    )(u, u, h, h, poolw_bd, pscale, dww, dwb, lng, lnb, pww, pwb)


def _mix_decode_kernel(u_ref, sp_ref, h_ref, sc_ref, poolw_ref, pscale_ref, dww_ref, dwb_ref,
                       lng_ref, lnb_ref, pww_ref, pwb_ref, pool_o, conv_o, *, start_pos):
    u = u_ref[...]
    sums = []
    run = u
    k = 1
    for wnd in POOL_WINDOWS:
        while k < wnd:
            run = run + sp_ref[POOL_HIST - k]
            k += 1
        sums.append(run)
    win = _pool_window_of_lane(u.shape)
    ssel = sums[-1]
    for g in reversed(range(len(POOL_WINDOWS) - 1)):
        ssel = jnp.where(win == POOL_WINDOWS[g], sums[g], ssel)
    cnt = jnp.minimum(win, start_pos + 1).astype(F32)
    d = (ssel / cnt - u).astype(BF16)
    pool_o[...] = (jnp.dot(d, poolw_ref[...], preferred_element_type=F32) * pscale_ref[...]).astype(BF16)

    y = h_ref[...] * dww_ref[CONV_HIST:CONV_TAPS, :] + dwb_ref[...]
    for tap in range(CONV_HIST):
        y = y + sc_ref[tap] * dww_ref[tap:tap + 1, :]
    conv_o[...] = _conv_tail(y, lng_ref[...], lnb_ref[...], pww_ref[...], pwb_ref[...]).astype(BF16)


def _mix_decode(u, state_pool_tm, h, state_conv_tm, poolw_bd, pscale, dww, dwb, lng, lnb, pww, pwb, *, start_pos):
    db, c = u.shape
    args = (u, state_pool_tm, h, state_conv_tm, poolw_bd, pscale, dww, dwb, lng, lnb, pww, pwb)
    return pl.pallas_call(
        functools.partial(_mix_decode_kernel, start_pos=start_pos),
        grid=(1,),
        in_specs=[_const_spec(a.shape) for a in args],
        out_specs=[_const_spec((db, c))] * 2,
        out_shape=[jax.ShapeDtypeStruct((db, c), BF16)] * 2,
        compiler_params=_params("arbitrary"),
        name="mix_decode",
    )(*args)


def _merge_ffn_kernel(x_ref, att_ref, pool_ref, conv_ref, wo_ref, g_ref, wup_ref, wdn_ref, o_ref, *, ff_chunk):
    cp = pool_ref.shape[1]
    if len(att_ref.shape) == 3:
        nh, dh, rows = att_ref.shape
        att = att_ref[...].reshape(nh * dh, rows).T.astype(BF16)
    else:
        att = att_ref[...]
    ca = att.shape[1]
    m = jnp.dot(att, wo_ref[0:ca, :], preferred_element_type=F32)
    m = m + jnp.dot(pool_ref[...], wo_ref[ca:ca + cp, :], preferred_element_type=F32)
    m = m + jnp.dot(conv_ref[...], wo_ref[ca + cp:, :], preferred_element_type=F32)
    x1 = x_ref[...] + m
    ms = jnp.mean(x1 * x1, axis=-1, keepdims=True)
    xn = (x1 * lax.rsqrt(ms + EPS) * g_ref[...]).astype(BF16)
    out = x1
    d_ff = wup_ref.shape[1]
    for c0 in range(0, d_ff, ff_chunk):
        hmid = jnp.dot(xn, wup_ref[:, c0:c0 + ff_chunk], preferred_element_type=F32)
        hmid = jnp.square(jnp.maximum(hmid, 0.0)).astype(BF16)
        out = out + jnp.dot(hmid, wdn_ref[c0:c0 + ff_chunk, :], preferred_element_type=F32)
    o_ref[...] = out


def _merge_ffn(x, att, pool, conv, wo_bf, g, wup_bf, wdn_bf, *, tm, ff_chunk):
    m, d = x.shape
    row = lambda c: pl.BlockSpec((tm, c), lambda i: (i, 0))
    single = lambda shape: pl.BlockSpec(shape, lambda i: (0, 0), pipeline_mode=pl.Buffered(1))
    if att.ndim == 4:
        nt = att.shape[3] // tm
        att_spec = pl.BlockSpec((None,) + att.shape[1:3] + (tm,), lambda i: (i // nt, 0, 0, i % nt))
    else:
        att_spec = row(att.shape[1])
    return pl.pallas_call(
        functools.partial(_merge_ffn_kernel, ff_chunk=ff_chunk),
        grid=(m // tm,),
        in_specs=[row(d), att_spec, row(pool.shape[1]), row(conv.shape[1]),
                  single(wo_bf.shape), single((1, d)), single(wup_bf.shape), single(wdn_bf.shape)],
        out_specs=row(d),
        out_shape=jax.ShapeDtypeStruct((m, d), F32),
        compiler_params=_params("parallel"),
        name="merge_ffn",
    )(x, att, pool, conv, wo_bf, g, wup_bf, wdn_bf)


def _block_diag(blocks):
    g, a, b = blocks.shape
    eye = jnp.eye(g, dtype=blocks.dtype)
    return (eye[:, None, :, None] * blocks[:, :, None, :]).reshape(g * a, g * b)


def _reverse_tri(n):
    return (jnp.arange(n)[None, :] >= jnp.arange(n)[:, None]).astype(BF16)


ATTN_TQ = 256
ATTN_HEADS_PER_STEP = 8
PROMPT_TM = 256
FFN_TM = 512
MIX_TT = 512
FF_CHUNK = 1024
DECODE_PAGES = 32
DECODE_PAGE_GROUP = 4


def kernel(x_prompt, x_sample, cache_k, cache_v, state_pool, state_conv, page_table, norm_mix_g, w_in, q_norm_g, k_norm_g, sb_bias, pool_w, pool_scale, conv_dw_w, conv_dw_b, conv_ln_g, conv_ln_b, conv_pw_w, conv_pw_b, w_out, norm_ffn_g, w_ffn_up, w_ffn_down):
    b, t, d = x_prompt.shape
    db, ts, _ = x_sample.shape
    assert ts == 1
    assert PROMPT_TM == ATTN_TQ
    depth = w_in.shape[0]
    n_heads = sb_bias.shape[1]
    attn_ch = n_heads * HEAD_DIM
    pool_ch = pool_scale.shape[1]
    conv_ch = conv_dw_b.shape[1]
    past_len = page_table.shape[1] * PAGE_SIZE

    xp = x_prompt.reshape(b * t, d)
    xs = x_sample.reshape(db, d)

    head_mean = _block_diag(jnp.full((n_heads, HEAD_DIM, HEAD_DIM), 1.0 / HEAD_DIM, BF16))
    tri_prompt = _reverse_tri(ATTN_TQ)
    tri_page = _reverse_tri(PAGE_SIZE).T
    cache_kt = cache_k.transpose(0, 1, 3, 4, 2)
    cache_vt = cache_v.transpose(0, 1, 3, 4, 2)
    ch = jnp.arange(attn_ch)
    place = (jnp.arange(n_heads * LANES)[None, :] == ((ch // HEAD_DIM) * LANES + ch % HEAD_DIM)[:, None]).astype(BF16)
    qpad = jnp.zeros((LANES - HEAD_DIM, PROMPT_TM), BF16).at[0:2, :].set(1.0)
    row2 = lambda v: v.reshape(1, -1)

    outs = {n: [] for n in ("ks", "vs", "pp", "ps", "cp", "cs")}
    kv_all = None
    for l in range(depth):
        poolw_bd = _block_diag(pool_w[l]).astype(BF16)
        mix_w = (poolw_bd, row2(pool_scale[l]), conv_dw_w[l], row2(conv_dw_b[l]),
                 row2(conv_ln_g[l]), row2(conv_ln_b[l]), conv_pw_w[l].astype(BF16), row2(conv_pw_b[l]))
        in_w = (row2(norm_mix_g[l]), w_in[l].astype(BF16), row2(jnp.tile(q_norm_g[l], n_heads)),
                row2(jnp.tile(k_norm_g[l], n_heads)), head_mean)
        dims = dict(attn_ch=attn_ch, pool_ch=pool_ch, conv_ch=conv_ch)
        ffn_w = (w_out[l].astype(BF16), row2(norm_ffn_g[l]), w_ffn_up[l].astype(BF16), w_ffn_down[l].astype(BF16))
        bias2 = sb_bias[l] * LOG2E
        bias_hi = bias2.astype(BF16)
        bias_lo = (bias2 - bias_hi.astype(F32)).astype(BF16)
        kpad = jnp.zeros((n_heads, LANES), F32).at[:, HEAD_DIM].set(bias_hi.astype(F32))
        kpad = kpad.at[:, HEAD_DIM + 1].set(bias_lo.astype(F32)).reshape(1, n_heads * LANES)

        qt, k_aug, vt, kt_all, vt_all, up, hp = _inproj_prompt(
            x_prompt if l == 0 else xp.reshape(b, t, d), *in_w, place, kpad, qpad, kv_all,
            layer=l, depth=depth, tm=PROMPT_TM, **dims)
        kv_all = (kt_all, vt_all)
        qs, ksf, vsf, us, hs = _inproj_decode(xs, *in_w, **dims)

        att_t = _attn_prompt(qt, k_aug, vt, tri_prompt, tq=ATTN_TQ, heads=ATTN_HEADS_PER_STEP)

        q_rep = jnp.broadcast_to(qs.reshape(db, n_heads, HEAD_DIM, 1), (db, n_heads, HEAD_DIM, PAGE_SIZE))
        bias_rep = jnp.broadcast_to(bias2[:, None], (n_heads, PAGE_SIZE))
        att_s = _attn_decode(page_table, q_rep, bias_rep, tri_page, cache_kt, cache_vt,
                             layer=l, pages=DECODE_PAGES).reshape(db, attn_ch).astype(BF16)

        pool_p, conv_p = _mix_prompt(up, hp, *mix_w, tt=MIX_TT)
        pool_s, conv_s = _mix_decode(us, state_pool[l].transpose(1, 0, 2), hs, state_conv[l].transpose(1, 0, 2),
                                     *mix_w, start_pos=past_len)

        xp = _merge_ffn(xp, att_t, pool_p.reshape(b * t, pool_ch), conv_p.reshape(b * t, conv_ch), *ffn_w,
                        tm=FFN_TM, ff_chunk=FF_CHUNK)
        xs = _merge_ffn(xs, att_s, pool_s, conv_s, *ffn_w, tm=db, ff_chunk=FF_CHUNK)

        outs["ks"].append(ksf.reshape(db, ts, n_heads, HEAD_DIM))
        outs["vs"].append(vsf.reshape(db, ts, n_heads, HEAD_DIM))
        outs["pp"].append(up[:, t - POOL_HIST:])
        outs["ps"].append(jnp.concatenate([state_pool[l][:, 1:], us[:, None, :]], axis=1))
        outs["cp"].append(hp[:, t - CONV_HIST:])
        outs["cs"].append(jnp.concatenate([state_conv[l][:, 1:], hs[:, None, :]], axis=1))

    return (xp.reshape(b, t, d), xs.reshape(db, ts, d),
            kv_all[0].transpose(0, 1, 4, 2, 3), kv_all[1].transpose(0, 1, 4, 2, 3),
            jnp.stack(outs["ks"]), jnp.stack(outs["vs"]),
            jnp.stack(outs["pp"]), jnp.stack(outs["ps"]), jnp.stack(outs["cp"]), jnp.stack(outs["cs"]))
```

```python
import functools

import jax
import jax.numpy as jnp
from jax import lax
from jax.experimental import pallas as pl
from jax.experimental.pallas import tpu as pltpu

F32 = jnp.float32
BF16 = jnp.bfloat16

EPS = 1e-6
LOG2E = 1.4426950408889634
HEAD_DIM = 64
POOL_WINDOWS = (2, 4, 8, 16)
POOL_HIST = max(POOL_WINDOWS) - 1
CONV_TAPS = 31
CONV_HIST = CONV_TAPS - 1
PAGE_SIZE = 128

V7X_VMEM_LIMIT_BYTES = 56 * 1024 * 1024
LANES = 128
SUBLANES = 8


def _params(*sem):
    return pltpu.CompilerParams(dimension_semantics=sem, vmem_limit_bytes=V7X_VMEM_LIMIT_BYTES)


def _const_spec(shape):
    nd = len(shape)
    return pl.BlockSpec(shape, lambda *_: (0,) * nd)


def _inproj_math(x, g, w, qg, kg, hm, *, attn_ch, pool_ch, conv_ch):
    ms = jnp.mean(x * x, axis=-1, keepdims=True)
    xn = (x * lax.rsqrt(ms + EPS) * g).astype(BF16)
    z = jnp.dot(xn, w, preferred_element_type=F32)
    c0, c1, c2 = attn_ch, 2 * attn_ch, 3 * attn_ch
    c3 = c2 + pool_ch
    c4 = c3 + conv_ch
    q, k, v = z[:, :c0], z[:, c0:c1], z[:, c1:c2]
    u, a, gate = z[:, c2:c3], z[:, c3:c4], z[:, c4:]

    def head_norm(t, gain):
        msq = jnp.dot((t * t).astype(BF16), hm, preferred_element_type=F32)
        return t * lax.rsqrt(msq + EPS) * gain

    q_scaled = head_norm(q, qg) * (HEAD_DIM ** -0.5 * LOG2E)
    return q_scaled, head_norm(k, kg), v, u, a * jax.nn.sigmoid(gate)


def _inproj_decode_kernel(x_ref, g_ref, w_ref, qg_ref, kg_ref, hm_ref,
                          q_ref, k_ref, v_ref, u_ref, h_ref, **dims):
    q, k, v, u, h = _inproj_math(x_ref[...], g_ref[...], w_ref[...], qg_ref[...], kg_ref[...], hm_ref[...], **dims)
    q_ref[...] = q
    k_ref[...] = k
    v_ref[...] = v
    u_ref[...] = u
    h_ref[...] = h


def _inproj_decode(x, g, w_bf, qg_t, kg_t, head_mean, *, attn_ch, pool_ch, conv_ch):
    m, d = x.shape
    widths = (attn_ch, attn_ch, attn_ch, pool_ch, conv_ch)
    args = (x, g, w_bf, qg_t, kg_t, head_mean)
    return pl.pallas_call(
        functools.partial(_inproj_decode_kernel, attn_ch=attn_ch, pool_ch=pool_ch, conv_ch=conv_ch),
        grid=(1,),
        in_specs=[_const_spec(a.shape) for a in args],
        out_specs=[_const_spec((m, c)) for c in widths],
        out_shape=[jax.ShapeDtypeStruct((m, c), F32) for c in widths],
        compiler_params=_params("arbitrary"),
        name="inproj_decode",
    )(*args)


def _inproj_prompt_kernel(x_ref, g_ref, w_ref, qg_ref, kg_ref, hm_ref, place_ref, kpad_ref, qpad_ref, *rest, **dims):
    qt_ref, ka_ref, vt_ref, kf_ref, vf_ref, u_ref, h_ref = rest[-7:]
    q, k, v, u, h = _inproj_math(x_ref[...], g_ref[...], w_ref[...], qg_ref[...], kg_ref[...], hm_ref[...], **dims)
    u_ref[...] = u
    h_ref[...] = h
    ka = (jnp.dot(k.astype(BF16), place_ref[...], preferred_element_type=F32) + kpad_ref[...]).astype(BF16)
    qt, kt, vt = q.T, k.T, v.T
    for hd in range(qt_ref.shape[0]):
        rows = slice(hd * HEAD_DIM, (hd + 1) * HEAD_DIM)
        qt_ref[hd, 0:HEAD_DIM, :] = qt[rows].astype(BF16)
        qt_ref[hd, HEAD_DIM:, :] = qpad_ref[...]
        ka_ref[hd] = ka[:, hd * LANES:(hd + 1) * LANES]
        vt_ref[hd] = vt[rows].astype(BF16)
        kf_ref[hd] = kt[rows]
        vf_ref[hd] = vt[rows]


def _inproj_prompt(x, g, w_bf, qg_t, kg_t, head_mean, place, kpad, qpad, kv_all, *,
                   layer, depth, tm, attn_ch, pool_ch, conv_ch):
    b, t, d = x.shape
    nt = t // tm
    nh = attn_ch // HEAD_DIM
    tok_minor = pl.BlockSpec((None, None, nh, HEAD_DIM, tm), lambda i: (layer, i // nt, 0, 0, i % nt))
    row = lambda c: pl.BlockSpec((None, tm, c), lambda i: (i // nt, i % nt, 0))
    consts = (g, w_bf, qg_t, kg_t, head_mean, place, kpad, qpad)
    carried = () if kv_all is None else tuple(kv_all)
    first_carried = 1 + len(consts)
    kv_shape = jax.ShapeDtypeStruct((depth, b, nh, HEAD_DIM, t), F32)
    return pl.pallas_call(
        functools.partial(_inproj_prompt_kernel, attn_ch=attn_ch, pool_ch=pool_ch, conv_ch=conv_ch),
        grid=(b * nt,),
        in_specs=[row(d)] + [_const_spec(a.shape) for a in consts] + [pl.BlockSpec(memory_space=pl.ANY)] * len(carried),
        out_specs=[
            pl.BlockSpec((None, nh, LANES, tm), lambda i: (i // nt, 0, 0, i % nt)),
            pl.BlockSpec((None, nh, None, tm, LANES), lambda i: (i // nt, 0, i % nt, 0, 0)),
            pl.BlockSpec((None, nh, None, HEAD_DIM, tm), lambda i: (i // nt, 0, i % nt, 0, 0)),
            tok_minor, tok_minor, row(pool_ch), row(conv_ch)],
        out_shape=[
            jax.ShapeDtypeStruct((b, nh, LANES, t), BF16),
            jax.ShapeDtypeStruct((b, nh, nt, tm, LANES), BF16),
            jax.ShapeDtypeStruct((b, nh, nt, HEAD_DIM, tm), BF16),
            kv_shape,
            kv_shape,
            jax.ShapeDtypeStruct((b, t, pool_ch), F32),
            jax.ShapeDtypeStruct((b, t, conv_ch), F32),
        ],
        input_output_aliases={first_carried + n: 3 + n for n in range(len(carried))},
        compiler_params=_params("parallel"),
        name="inproj_prompt",
    )(x, *consts, *carried)


SOFTPLUS2_CLAMP = 100.0


def _softplus2(z2):
    return jnp.maximum(z2, jnp.log(1.0 + jnp.exp2(jnp.minimum(z2, SOFTPLUS2_CLAMP))) * LOG2E)


def _attn_prompt_kernel(qt_ref, k_ref, vt_ref, tri_ref, o_ref, r_ref, za_ref, zb_ref, zc_ref):
    i = pl.program_id(2)
    heads = qt_ref.shape[0]
    tri = tri_ref[...]
    tk, tq = tri.shape[0], qt_ref.shape[-1]

    def scores(j, z_ref):
        for g in range(heads):
            z_ref[g] = jnp.dot(k_ref[g, j], qt_ref[g], preferred_element_type=F32)

    def step(j, zcur_ref, zn_ref, diagonal=False):
        scores(jnp.maximum(j - 1, 0), zn_ref)
        if diagonal:
            vis = lax.broadcasted_iota(jnp.int32, (tk, tq), 0) < lax.broadcasted_iota(jnp.int32, (tk, tq), 1)
            sps = [jnp.where(vis, _softplus2(zcur_ref[g]), 0.0) for g in range(heads)]
        else:
            sps = [_softplus2(zcur_ref[g]) for g in range(heads)]
        ss = [jnp.dot(tri, sp.astype(BF16), preferred_element_type=F32) for sp in sps]
        for g in range(heads):
            w = jnp.exp2(zcur_ref[g] - ss[g])
            if diagonal:
                w = jnp.where(vis, w, 0.0)
            pv = jnp.dot(vt_ref[g, j], w.astype(BF16), preferred_element_type=F32)
            if diagonal:
                o_ref[g] = pv
                r_ref[g] = ss[g][0:1, :]
            else:
                r = r_ref[g]
                o_ref[g] += pv * jnp.exp2(-r)
                r_ref[g] = r + ss[g][0:1, :]

    scores(i, za_ref)
    step(i, za_ref, zb_ref, diagonal=True)

    def body(p, carry):
        j = i - 1 - 3 * p
        step(j, zb_ref, zc_ref)
        step(j - 1, zc_ref, za_ref)
        step(j - 2, za_ref, zb_ref)
        return carry

    lax.fori_loop(0, i // 3, body, 0)
    rem = i % 3

    @pl.when(rem >= 1)
    def _():
        step(rem - 1, zb_ref, zc_ref)

    @pl.when(rem == 2)
    def _():
        step(0, zc_ref, za_ref)


def _attn_prompt(qt, k_hm, vt_hm, tri, *, tq, heads):
    b, h, da, t = qt.shape
    nk, tk = k_hm.shape[2], k_hm.shape[3]
    dh = vt_hm.shape[3]
    return pl.pallas_call(
        _attn_prompt_kernel,
        grid=(b, h // heads, t // tq),
        in_specs=[
            pl.BlockSpec((None, heads, da, tq), lambda bi, hi, i: (bi, hi, 0, i)),
            pl.BlockSpec((None, heads, nk, tk, da), lambda bi, hi, i: (bi, hi, 0, 0, 0), pipeline_mode=pl.Buffered(1)),
            pl.BlockSpec((None, heads, nk, dh, tk), lambda bi, hi, i: (bi, hi, 0, 0, 0), pipeline_mode=pl.Buffered(1)),
            _const_spec((tk, tk)),
        ],
        out_specs=pl.BlockSpec((None, heads, dh, tq), lambda bi, hi, i: (bi, hi, 0, i)),
        out_shape=jax.ShapeDtypeStruct((b, h, dh, t), F32),
        scratch_shapes=[pltpu.VMEM((heads, 1, tq), F32)] + [pltpu.VMEM((heads, tk, tq), F32)] * 3,
        compiler_params=_params("parallel", "parallel", "arbitrary"),
        name="attn_prompt",
    )(qt, k_hm, vt_hm, tri)


def _attn_decode_kernel(pt_ref, q_ref, bias_ref, tri_ref, *refs, pages):
    del pt_ref
    k_refs, v_refs = refs[:pages], refs[pages:2 * pages]
    o_ref, acc_ref, r_ref = refs[2 * pages:]
    j = pl.program_id(1)

    @pl.when(j == 0)
    def _():
        acc_ref[...] = jnp.zeros_like(acc_ref)
        r_ref[...] = jnp.zeros_like(r_ref)

    bias = bias_ref[...]
    tri = tri_ref[...]
    r = r_ref[...]
    nh = q_ref.shape[0]
    order = list(reversed(range(pages)))
    for g0 in range(0, pages, DECODE_PAGE_GROUP):
        group = order[g0:g0 + DECODE_PAGE_GROUP]
        zrows = [[] for _ in group]
        for h in range(nh):
            qh = q_ref[h]
            for n, p in enumerate(group):
                zrows[n].append(jnp.sum(k_refs[p][h] * qh, axis=0, keepdims=True))
        ws = []
        for n in range(len(group)):
            z = jnp.concatenate(zrows[n], axis=0) + bias
            s = jnp.dot(_softplus2(z).astype(BF16), tri, preferred_element_type=F32)
            ws.append(jnp.exp2(z - s - r))
            r = r + s[:, 0:1]
        for h in range(nh):
            acc = acc_ref[h]
            for n, p in enumerate(group):
                acc = acc + ws[n][h:h + 1, :] * v_refs[p][h]
            acc_ref[h] = acc
    r_ref[...] = r

    @pl.when(j == pl.num_programs(1) - 1)
    def _():
        o_ref[...] = jnp.sum(acc_ref[...], axis=-1)


def _attn_decode(page_table, q_rep, bias_rep, tri, cache_kt, cache_vt, *, layer, pages):
    db, n_pages = page_table.shape
    nh, dh, page = cache_kt.shape[2:]
    groups = n_pages // pages

    def page_spec(p):
        return pl.BlockSpec(
            (None, None, nh, dh, page),
            lambda b, j, pt: (layer, pt[b, (groups - 1 - j) * pages + p], 0, 0, 0))

    grid_spec = pltpu.PrefetchScalarGridSpec(
        num_scalar_prefetch=1,
        grid=(db, groups),
        in_specs=[
            pl.BlockSpec((None, nh, dh, page), lambda b, j, pt: (b, 0, 0, 0)),
            pl.BlockSpec((nh, page), lambda b, j, pt: (0, 0)),
            pl.BlockSpec((page, page), lambda b, j, pt: (0, 0)),
        ] + [page_spec(p) for p in range(pages)] * 2,
        out_specs=pl.BlockSpec((None, nh, dh), lambda b, j, pt: (b, 0, 0)),
        scratch_shapes=[pltpu.VMEM((nh, dh, page), F32), pltpu.VMEM((nh, 1), F32)],
    )
    return pl.pallas_call(
        functools.partial(_attn_decode_kernel, pages=pages),
        grid_spec=grid_spec,
        out_shape=jax.ShapeDtypeStruct((db, nh, dh), F32),
        compiler_params=_params("parallel", "arbitrary"),
        name="attn_decode",
    )(page_table, q_rep, bias_rep, tri, *([cache_kt] * pages), *([cache_vt] * pages))


def _conv_tail(y, lng, lnb, pww, pwb):
    mu = jnp.mean(y, axis=-1, keepdims=True)
    yc = y - mu
    var = jnp.mean(yc * yc, axis=-1, keepdims=True)
    yn = yc * lax.rsqrt(var + EPS) * lng + lnb
    yn = (yn * jax.nn.sigmoid(yn)).astype(BF16)
    return jnp.dot(yn, pww, preferred_element_type=F32) + pwb


def _pool_window_of_lane(shape):
    lane = lax.broadcasted_iota(jnp.int32, shape, len(shape) - 1)
    gw = shape[-1] // len(POOL_WINDOWS)
    win = jnp.full(shape, POOL_WINDOWS[-1], jnp.int32)
    for g in reversed(range(len(POOL_WINDOWS) - 1)):
        win = jnp.where(lane < (g + 1) * gw, POOL_WINDOWS[g], win)
    return win


HALO = 32


def _mix_prompt_kernel(u_ref, up_ref, h_ref, hp_ref, poolw_ref, pscale_ref, dww_ref, dwb_ref,
                       lng_ref, lnb_ref, pww_ref, pwb_ref, pool_o, conv_o, uext, hext, ush, hsh):
    i = pl.program_id(1)
    tt, c = u_ref.shape

    @pl.when(i == 0)
    def _():
        uext[0:HALO, :] = jnp.zeros((HALO, c), F32)
        hext[0:HALO, :] = jnp.zeros((HALO, c), F32)

    @pl.when(i > 0)
    def _():
        uext[0:HALO, :] = up_ref[...]
        hext[0:HALO, :] = hp_ref[...]

    u = u_ref[...]
    uext[HALO:, :] = u
    hext[HALO:, :] = h_ref[...]

    for off in range(1, SUBLANES):
        ush[off - 1] = uext[pl.ds(off, HALO + tt - SUBLANES), :]
        hsh[off - 1] = hext[pl.ds(off, HALO + tt - SUBLANES), :]

    def window(ext, shifted, r0):
        a, off = divmod(r0, SUBLANES)
        if off == 0:
            return ext[pl.ds(r0, tt), :]
        return shifted[off - 1, pl.ds(a * SUBLANES, tt), :]

    sums = []
    run = u
    k = 1
    for wnd in POOL_WINDOWS:
        while k < wnd:
            run = run + window(uext, ush, HALO - k)
            k += 1
        sums.append(run)
    win = _pool_window_of_lane((tt, c))
    ssel = sums[-1]
    for g in reversed(range(len(POOL_WINDOWS) - 1)):
        ssel = jnp.where(win == POOL_WINDOWS[g], sums[g], ssel)
    pos = i * tt + lax.broadcasted_iota(jnp.int32, (tt, c), 0)
    cnt = jnp.minimum(win, pos + 1).astype(F32)
    d = (ssel / cnt - u).astype(BF16)
    pool_o[...] = (jnp.dot(d, poolw_ref[...], preferred_element_type=F32) * pscale_ref[...]).astype(BF16)

    y = jnp.zeros((tt, c), F32) + dwb_ref[...]
    for tap in range(CONV_TAPS):
        y = y + window(hext, hsh, HALO - CONV_HIST + tap) * dww_ref[tap:tap + 1, :]
    conv_o[...] = _conv_tail(y, lng_ref[...], lnb_ref[...], pww_ref[...], pwb_ref[...]).astype(BF16)


def _mix_prompt(u, h, poolw_bd, pscale, dww, dwb, lng, lnb, pww, pwb, *, tt):
    b, t, c = u.shape
    r = tt // HALO
    cur = pl.BlockSpec((None, tt, c), lambda bi, i: (bi, i, 0))
    prev = pl.BlockSpec((None, HALO, c), lambda bi, i: (bi, jnp.maximum(i * r - 1, 0), 0))
    vec = _const_spec((1, c))
    return pl.pallas_call(
        _mix_prompt_kernel,
        grid=(b, t // tt),
        in_specs=[cur, prev, cur, prev, _const_spec((c, c)), vec, _const_spec(dww.shape), vec,
                  vec, vec, _const_spec((c, c)), vec],
        out_specs=[cur, cur],
        out_shape=[jax.ShapeDtypeStruct((b, t, c), BF16)] * 2,
        scratch_shapes=[pltpu.VMEM((HALO + tt, c), F32)] * 2
        + [pltpu.VMEM((SUBLANES - 1, HALO + tt - SUBLANES, c), F32)] * 2,
        compiler_params=_params("parallel", "arbitrary"),
        name="mix_prompt",
    )(u, u, h, h, poolw_bd, pscale, dww, dwb, lng, lnb, pww, pwb)


def _mix_decode_kernel(u_ref, sp_ref, h_ref, sc_ref, poolw_ref, pscale_ref, dww_ref, dwb_ref,
                       lng_ref, lnb_ref, pww_ref, pwb_ref, pool_o, conv_o, *, start_pos):
    u = u_ref[...]
    sums = []
    run = u
    k = 1
    for wnd in POOL_WINDOWS:
        while k < wnd:
            run = run + sp_ref[POOL_HIST - k]
            k += 1
        sums.append(run)
    win = _pool_window_of_lane(u.shape)
    ssel = sums[-1]
    for g in reversed(range(len(POOL_WINDOWS) - 1)):
        ssel = jnp.where(win == POOL_WINDOWS[g], sums[g], ssel)
    cnt = jnp.minimum(win, start_pos + 1).astype(F32)
    d = (ssel / cnt - u).astype(BF16)
    pool_o[...] = (jnp.dot(d, poolw_ref[...], preferred_element_type=F32) * pscale_ref[...]).astype(BF16)

    y = h_ref[...] * dww_ref[CONV_HIST:CONV_TAPS, :] + dwb_ref[...]
    for tap in range(CONV_HIST):
        y = y + sc_ref[tap] * dww_ref[tap:tap + 1, :]
    conv_o[...] = _conv_tail(y, lng_ref[...], lnb_ref[...], pww_ref[...], pwb_ref[...]).astype(BF16)


def _mix_decode(u, state_pool_tm, h, state_conv_tm, poolw_bd, pscale, dww, dwb, lng, lnb, pww, pwb, *, start_pos):
    db, c = u.shape
    args = (u, state_pool_tm, h, state_conv_tm, poolw_bd, pscale, dww, dwb, lng, lnb, pww, pwb)
    return pl.pallas_call(
        functools.partial(_mix_decode_kernel, start_pos=start_pos),
        grid=(1,),
        in_specs=[_const_spec(a.shape) for a in args],
        out_specs=[_const_spec((db, c))] * 2,
        out_shape=[jax.ShapeDtypeStruct((db, c), BF16)] * 2,
        compiler_params=_params("arbitrary"),
        name="mix_decode",
    )(*args)


def _merge_ffn_kernel(x_ref, att_ref, pool_ref, conv_ref, wo_ref, g_ref, wup_ref, wdn_ref, o_ref, *, ff_chunk):
    cp = pool_ref.shape[1]
    if len(att_ref.shape) == 3:
        nh, dh, rows = att_ref.shape
        att = att_ref[...].reshape(nh * dh, rows).T.astype(BF16)
    else:
        att = att_ref[...]
    ca = att.shape[1]
    m = jnp.dot(att, wo_ref[0:ca, :], preferred_element_type=F32)
    m = m + jnp.dot(pool_ref[...], wo_ref[ca:ca + cp, :], preferred_element_type=F32)
    m = m + jnp.dot(conv_ref[...], wo_ref[ca + cp:, :], preferred_element_type=F32)
    x1 = x_ref[...] + m
    ms = jnp.mean(x1 * x1, axis=-1, keepdims=True)
    xn = (x1 * lax.rsqrt(ms + EPS) * g_ref[...]).astype(BF16)
    out = x1
    d_ff = wup_ref.shape[1]
    for c0 in range(0, d_ff, ff_chunk):
        hmid = jnp.dot(xn, wup_ref[:, c0:c0 + ff_chunk], preferred_element_type=F32)
        hmid = jnp.square(jnp.maximum(hmid, 0.0)).astype(BF16)
        out = out + jnp.dot(hmid, wdn_ref[c0:c0 + ff_chunk, :], preferred_element_type=F32)
    o_ref[...] = out


def _merge_ffn(x, att, pool, conv, wo_bf, g, wup_bf, wdn_bf, *, tm, ff_chunk):
    m, d = x.shape
    row = lambda c: pl.BlockSpec((tm, c), lambda i: (i, 0))
    single = lambda shape: pl.BlockSpec(shape, lambda i: (0, 0), pipeline_mode=pl.Buffered(1))
    if att.ndim == 4:
        nt = att.shape[3] // tm
        att_spec = pl.BlockSpec((None,) + att.shape[1:3] + (tm,), lambda i: (i // nt, 0, 0, i % nt))
    else:
        att_spec = row(att.shape[1])
    return pl.pallas_call(
        functools.partial(_merge_ffn_kernel, ff_chunk=ff_chunk),
        grid=(m // tm,),
        in_specs=[row(d), att_spec, row(pool.shape[1]), row(conv.shape[1]),
                  single(wo_bf.shape), single((1, d)), single(wup_bf.shape), single(wdn_bf.shape)],
        out_specs=row(d),
        out_shape=jax.ShapeDtypeStruct((m, d), F32),
        compiler_params=_params("parallel"),
        name="merge_ffn",
    )(x, att, pool, conv, wo_bf, g, wup_bf, wdn_bf)


def _block_diag(blocks):
    g, a, b = blocks.shape
    eye = jnp.eye(g, dtype=blocks.dtype)
    return (eye[:, None, :, None] * blocks[:, :, None, :]).reshape(g * a, g * b)


def _reverse_tri(n):
    return (jnp.arange(n)[None, :] >= jnp.arange(n)[:, None]).astype(BF16)


ATTN_TQ = 256
ATTN_HEADS_PER_STEP = 8
PROMPT_TM = 256
FFN_TM = 512
MIX_TT = 512
FF_CHUNK = 1024
DECODE_PAGES = 32
DECODE_PAGE_GROUP = 4


def kernel(x_prompt, x_sample, cache_k, cache_v, state_pool, state_conv, page_table, norm_mix_g, w_in, q_norm_g, k_norm_g, sb_bias, pool_w, pool_scale, conv_dw_w, conv_dw_b, conv_ln_g, conv_ln_b, conv_pw_w, conv_pw_b, w_out, norm_ffn_g, w_ffn_up, w_ffn_down):
    b, t, d = x_prompt.shape
    db, ts, _ = x_sample.shape
    assert ts == 1
    assert PROMPT_TM == ATTN_TQ
    depth = w_in.shape[0]
    n_heads = sb_bias.shape[1]
    attn_ch = n_heads * HEAD_DIM
    pool_ch = pool_scale.shape[1]
    conv_ch = conv_dw_b.shape[1]
    past_len = page_table.shape[1] * PAGE_SIZE

    xp = x_prompt.reshape(b * t, d)
    xs = x_sample.reshape(db, d)

    head_mean = _block_diag(jnp.full((n_heads, HEAD_DIM, HEAD_DIM), 1.0 / HEAD_DIM, BF16))
    tri_prompt = _reverse_tri(ATTN_TQ)
    tri_page = _reverse_tri(PAGE_SIZE).T
    cache_kt = cache_k.transpose(0, 1, 3, 4, 2)
    cache_vt = cache_v.transpose(0, 1, 3, 4, 2)
    ch = jnp.arange(attn_ch)
    place = (jnp.arange(n_heads * LANES)[None, :] == ((ch // HEAD_DIM) * LANES + ch % HEAD_DIM)[:, None]).astype(BF16)
    qpad = jnp.zeros((LANES - HEAD_DIM, PROMPT_TM), BF16).at[0:2, :].set(1.0)
    row2 = lambda v: v.reshape(1, -1)

    outs = {n: [] for n in ("ks", "vs", "pp", "ps", "cp", "cs")}
    kv_all = None
    for l in range(depth):
        poolw_bd = _block_diag(pool_w[l]).astype(BF16)
        mix_w = (poolw_bd, row2(pool_scale[l]), conv_dw_w[l], row2(conv_dw_b[l]),
                 row2(conv_ln_g[l]), row2(conv_ln_b[l]), conv_pw_w[l].astype(BF16), row2(conv_pw_b[l]))
        in_w = (row2(norm_mix_g[l]), w_in[l].astype(BF16), row2(jnp.tile(q_norm_g[l], n_heads)),
                row2(jnp.tile(k_norm_g[l], n_heads)), head_mean)
        dims = dict(attn_ch=attn_ch, pool_ch=pool_ch, conv_ch=conv_ch)
        ffn_w = (w_out[l].astype(BF16), row2(norm_ffn_g[l]), w_ffn_up[l].astype(BF16), w_ffn_down[l].astype(BF16))
        bias2 = sb_bias[l] * LOG2E
        bias_hi = bias2.astype(BF16)
        bias_lo = (bias2 - bias_hi.astype(F32)).astype(BF16)
        kpad = jnp.zeros((n_heads, LANES), F32).at[:, HEAD_DIM].set(bias_hi.astype(F32))
        kpad = kpad.at[:, HEAD_DIM + 1].set(bias_lo.astype(F32)).reshape(1, n_heads * LANES)

        qt, k_aug, vt, kt_all, vt_all, up, hp = _inproj_prompt(
            x_prompt if l == 0 else xp.reshape(b, t, d), *in_w, place, kpad, qpad, kv_all,
            layer=l, depth=depth, tm=PROMPT_TM, **dims)
        kv_all = (kt_all, vt_all)
        qs, ksf, vsf, us, hs = _inproj_decode(xs, *in_w, **dims)

        att_t = _attn_prompt(qt, k_aug, vt, tri_prompt, tq=ATTN_TQ, heads=ATTN_HEADS_PER_STEP)

        q_rep = jnp.broadcast_to(qs.reshape(db, n_heads, HEAD_DIM, 1), (db, n_heads, HEAD_DIM, PAGE_SIZE))
        bias_rep = jnp.broadcast_to(bias2[:, None], (n_heads, PAGE_SIZE))
        att_s = _attn_decode(page_table, q_rep, bias_rep, tri_page, cache_kt, cache_vt,
                             layer=l, pages=DECODE_PAGES).reshape(db, attn_ch).astype(BF16)

        pool_p, conv_p = _mix_prompt(up, hp, *mix_w, tt=MIX_TT)
        pool_s, conv_s = _mix_decode(us, state_pool[l].transpose(1, 0, 2), hs, state_conv[l].transpose(1, 0, 2),
                                     *mix_w, start_pos=past_len)

        xp = _merge_ffn(xp, att_t, pool_p.reshape(b * t, pool_ch), conv_p.reshape(b * t, conv_ch), *ffn_w,
                        tm=FFN_TM, ff_chunk=FF_CHUNK)
        xs = _merge_ffn(xs, att_s, pool_s, conv_s, *ffn_w, tm=db, ff_chunk=FF_CHUNK)

        outs["ks"].append(ksf.reshape(db, ts, n_heads, HEAD_DIM))
        outs["vs"].append(vsf.reshape(db, ts, n_heads, HEAD_DIM))
        outs["pp"].append(up[:, t - POOL_HIST:])
        outs["ps"].append(jnp.concatenate([state_pool[l][:, 1:], us[:, None, :]], axis=1))
        outs["cp"].append(hp[:, t - CONV_HIST:])
        outs["cs"].append(jnp.concatenate([state_conv[l][:, 1:], hs[:, None, :]], axis=1))

    return (xp.reshape(b, t, d), xs.reshape(db, ts, d),
            kv_all[0].transpose(0, 1, 4, 2, 3), kv_all[1].transpose(0, 1, 4, 2, 3),
            jnp.stack(outs["ks"]), jnp.stack(outs["vs"]),
            jnp.stack(outs["pp"]), jnp.stack(outs["ps"]), jnp.stack(outs["cp"]), jnp.stack(outs["cs"]))
```

```python
import functools

import jax
import jax.numpy as jnp
from jax import lax
from jax.experimental import pallas as pl
from jax.experimental.pallas import tpu as pltpu

F32 = jnp.float32
BF16 = jnp.bfloat16

EPS = 1e-6
LOG2E = 1.4426950408889634
HEAD_DIM = 64
POOL_WINDOWS = (2, 4, 8, 16)
POOL_HIST = max(POOL_WINDOWS) - 1
CONV_TAPS = 31
CONV_HIST = CONV_TAPS - 1
PAGE_SIZE = 128

V7X_VMEM_LIMIT_BYTES = 56 * 1024 * 1024
LANES = 128
SUBLANES = 8


def _params(*sem):
    return pltpu.CompilerParams(dimension_semantics=sem, vmem_limit_bytes=V7X_VMEM_LIMIT_BYTES)


def _const_spec(shape):
    nd = len(shape)
    return pl.BlockSpec(shape, lambda *_: (0,) * nd)


def _inproj_math(x, g, w, qg, kg, hm, *, attn_ch, pool_ch, conv_ch):
    ms = jnp.mean(x * x, axis=-1, keepdims=True)
    xn = (x * lax.rsqrt(ms + EPS) * g).astype(BF16)
    z = jnp.dot(xn, w, preferred_element_type=F32)
    c0, c1, c2 = attn_ch, 2 * attn_ch, 3 * attn_ch
    c3 = c2 + pool_ch
    c4 = c3 + conv_ch
    q, k, v = z[:, :c0], z[:, c0:c1], z[:, c1:c2]
    u, a, gate = z[:, c2:c3], z[:, c3:c4], z[:, c4:]

    def head_norm(t, gain):
        msq = jnp.dot((t * t).astype(BF16), hm, preferred_element_type=F32)
        return t * lax.rsqrt(msq + EPS) * gain

    q_scaled = head_norm(q, qg) * (HEAD_DIM ** -0.5 * LOG2E)
    return q_scaled, head_norm(k, kg), v, u, a * jax.nn.sigmoid(gate)


def _inproj_decode_kernel(x_ref, g_ref, w_ref, qg_ref, kg_ref, hm_ref,
                          q_ref, k_ref, v_ref, u_ref, h_ref, **dims):
    q, k, v, u, h = _inproj_math(x_ref[...], g_ref[...], w_ref[...], qg_ref[...], kg_ref[...], hm_ref[...], **dims)
    q_ref[...] = q
    k_ref[...] = k
    v_ref[...] = v
    u_ref[...] = u
    h_ref[...] = h


def _inproj_decode(x, g, w_bf, qg_t, kg_t, head_mean, *, attn_ch, pool_ch, conv_ch):
    m, d = x.shape
    widths = (attn_ch, attn_ch, attn_ch, pool_ch, conv_ch)
    args = (x, g, w_bf, qg_t, kg_t, head_mean)
    return pl.pallas_call(
        functools.partial(_inproj_decode_kernel, attn_ch=attn_ch, pool_ch=pool_ch, conv_ch=conv_ch),
        grid=(1,),
        in_specs=[_const_spec(a.shape) for a in args],
        out_specs=[_const_spec((m, c)) for c in widths],
        out_shape=[jax.ShapeDtypeStruct((m, c), F32) for c in widths],
        compiler_params=_params("arbitrary"),
        name="inproj_decode",
    )(*args)


def _inproj_prompt_kernel(x_ref, g_ref, w_ref, qg_ref, kg_ref, hm_ref, place_ref, kpad_ref, qpad_ref, *rest, **dims):
    qt_ref, ka_ref, vt_ref, kf_ref, vf_ref, u_ref, h_ref = rest[-7:]
    q, k, v, u, h = _inproj_math(x_ref[...], g_ref[...], w_ref[...], qg_ref[...], kg_ref[...], hm_ref[...], **dims)
    u_ref[...] = u
    h_ref[...] = h
    ka = (jnp.dot(k.astype(BF16), place_ref[...], preferred_element_type=F32) + kpad_ref[...]).astype(BF16)
    qt, kt, vt = q.T, k.T, v.T
    for hd in range(qt_ref.shape[0]):
        rows = slice(hd * HEAD_DIM, (hd + 1) * HEAD_DIM)
        qt_ref[hd, 0:HEAD_DIM, :] = qt[rows].astype(BF16)
        qt_ref[hd, HEAD_DIM:, :] = qpad_ref[...]
        ka_ref[hd] = ka[:, hd * LANES:(hd + 1) * LANES]
        vt_ref[hd] = vt[rows].astype(BF16)
        kf_ref[hd] = kt[rows]
        vf_ref[hd] = vt[rows]


def _inproj_prompt(x, g, w_bf, qg_t, kg_t, head_mean, place, kpad, qpad, kv_all, *,
                   layer, depth, tm, attn_ch, pool_ch, conv_ch):
    b, t, d = x.shape
    nt = t // tm
    nh = attn_ch // HEAD_DIM
    tok_minor = pl.BlockSpec((None, None, nh, HEAD_DIM, tm), lambda i: (layer, i // nt, 0, 0, i % nt))
    row = lambda c: pl.BlockSpec((None, tm, c), lambda i: (i // nt, i % nt, 0))
    consts = (g, w_bf, qg_t, kg_t, head_mean, place, kpad, qpad)
    carried = () if kv_all is None else tuple(kv_all)
    first_carried = 1 + len(consts)
    kv_shape = jax.ShapeDtypeStruct((depth, b, nh, HEAD_DIM, t), F32)
    return pl.pallas_call(
        functools.partial(_inproj_prompt_kernel, attn_ch=attn_ch, pool_ch=pool_ch, conv_ch=conv_ch),
        grid=(b * nt,),
        in_specs=[row(d)] + [_const_spec(a.shape) for a in consts] + [pl.BlockSpec(memory_space=pl.ANY)] * len(carried),
        out_specs=[
            pl.BlockSpec((None, nh, LANES, tm), lambda i: (i // nt, 0, 0, i % nt)),
            pl.BlockSpec((None, nh, None, tm, LANES), lambda i: (i // nt, 0, i % nt, 0, 0)),
            pl.BlockSpec((None, nh, None, HEAD_DIM, tm), lambda i: (i // nt, 0, i % nt, 0, 0)),
            tok_minor, tok_minor, row(pool_ch), row(conv_ch)],
        out_shape=[
            jax.ShapeDtypeStruct((b, nh, LANES, t), BF16),
            jax.ShapeDtypeStruct((b, nh, nt, tm, LANES), BF16),
            jax.ShapeDtypeStruct((b, nh, nt, HEAD_DIM, tm), BF16),
            kv_shape,
            kv_shape,
            jax.ShapeDtypeStruct((b, t, pool_ch), F32),
            jax.ShapeDtypeStruct((b, t, conv_ch), F32),
        ],
        input_output_aliases={first_carried + n: 3 + n for n in range(len(carried))},
        compiler_params=_params("parallel"),
        name="inproj_prompt",
    )(x, *consts, *carried)


SOFTPLUS2_CLAMP = 100.0


def _softplus2(z2):
    return jnp.maximum(z2, jnp.log(1.0 + jnp.exp2(jnp.minimum(z2, SOFTPLUS2_CLAMP))) * LOG2E)


def _attn_prompt_kernel(qt_ref, k_ref, vt_ref, tri_ref, o_ref, r_ref, za_ref, zb_ref):
    i = pl.program_id(2)
    heads = qt_ref.shape[0]
    tri = tri_ref[...]
    tk, tq = tri.shape[0], qt_ref.shape[-1]

    def scores(j, z_ref):
        for g in range(heads):
            z_ref[g] = jnp.dot(k_ref[g, j], qt_ref[g], preferred_element_type=F32)

    def step(j, zc_ref, zn_ref, diagonal=False):
        scores(jnp.maximum(j - 1, 0), zn_ref)
        if diagonal:
            vis = lax.broadcasted_iota(jnp.int32, (tk, tq), 0) < lax.broadcasted_iota(jnp.int32, (tk, tq), 1)
            sps = [jnp.where(vis, _softplus2(zc_ref[g]), 0.0) for g in range(heads)]
        else:
            sps = [_softplus2(zc_ref[g]) for g in range(heads)]
        ss = [jnp.dot(tri, sp.astype(BF16), preferred_element_type=F32) for sp in sps]
        for g in range(heads):
            w = jnp.exp2(jnp.dot(k_ref[g, j], qt_ref[g], preferred_element_type=F32) - ss[g])
            if diagonal:
                w = jnp.where(vis, w, 0.0)
            pv = jnp.dot(vt_ref[g, j], w.astype(BF16), preferred_element_type=F32)
            if diagonal:
                o_ref[g] = pv
                r_ref[g] = ss[g][0:1, :]
            else:
                r = r_ref[g]
                o_ref[g] += pv * jnp.exp2(-r)
                r_ref[g] = r + ss[g][0:1, :]

    scores(i, za_ref)
    step(i, za_ref, zb_ref, diagonal=True)

    def body(p, carry):
        j = i - 1 - 2 * p
        step(j, zb_ref, za_ref)
        step(j - 1, za_ref, zb_ref)
        return carry

    lax.fori_loop(0, i // 2, body, 0)

    @pl.when(i % 2 == 1)
    def _():
        step(0, zb_ref, za_ref)


def _attn_prompt(qt, k_hm, vt_hm, tri, *, tq, heads):
    b, h, da, t = qt.shape
    nk, tk = k_hm.shape[2], k_hm.shape[3]
    dh = vt_hm.shape[3]
    return pl.pallas_call(
        _attn_prompt_kernel,
        grid=(b, h // heads, t // tq),
        in_specs=[
            pl.BlockSpec((None, heads, da, tq), lambda bi, hi, i: (bi, hi, 0, i)),
            pl.BlockSpec((None, heads, nk, tk, da), lambda bi, hi, i: (bi, hi, 0, 0, 0), pipeline_mode=pl.Buffered(1)),
            pl.BlockSpec((None, heads, nk, dh, tk), lambda bi, hi, i: (bi, hi, 0, 0, 0), pipeline_mode=pl.Buffered(1)),
            _const_spec((tk, tk)),
        ],
        out_specs=pl.BlockSpec((None, heads, dh, tq), lambda bi, hi, i: (bi, hi, 0, i)),
        out_shape=jax.ShapeDtypeStruct((b, h, dh, t), F32),
        scratch_shapes=[pltpu.VMEM((heads, 1, tq), F32)] + [pltpu.VMEM((heads, tk, tq), F32)] * 2,
        compiler_params=_params("parallel", "parallel", "arbitrary"),
        name="attn_prompt",
    )(qt, k_hm, vt_hm, tri)


def _attn_decode_kernel(pt_ref, q_ref, bias_ref, tri_ref, *refs, pages):
    del pt_ref
    k_refs, v_refs = refs[:pages], refs[pages:2 * pages]
    o_ref, acc_ref, r_ref = refs[2 * pages:]
    j = pl.program_id(1)

    @pl.when(j == 0)
    def _():
        acc_ref[...] = jnp.zeros_like(acc_ref)
        r_ref[...] = jnp.zeros_like(r_ref)

    bias = bias_ref[...]
    tri = tri_ref[...]
    r = r_ref[...]
    nh = q_ref.shape[0]
    order = list(reversed(range(pages)))
    for g0 in range(0, pages, DECODE_PAGE_GROUP):
        group = order[g0:g0 + DECODE_PAGE_GROUP]
        zrows = [[] for _ in group]
        for h in range(nh):
            qh = q_ref[h]
            for n, p in enumerate(group):
                zrows[n].append(jnp.sum(k_refs[p][h] * qh, axis=0, keepdims=True))
        ws = []
        for n in range(len(group)):
            z = jnp.concatenate(zrows[n], axis=0) + bias
            s = jnp.dot(_softplus2(z).astype(BF16), tri, preferred_element_type=F32)
            ws.append(jnp.exp2(z - s - r))
            r = r + s[:, 0:1]
        for h in range(nh):
            acc = acc_ref[h]
            for n, p in enumerate(group):
                acc = acc + ws[n][h:h + 1, :] * v_refs[p][h]
            acc_ref[h] = acc
    r_ref[...] = r

    @pl.when(j == pl.num_programs(1) - 1)
    def _():
        o_ref[...] = jnp.sum(acc_ref[...], axis=-1)


def _attn_decode(page_table, q_rep, bias_rep, tri, cache_kt, cache_vt, *, layer, pages):
    db, n_pages = page_table.shape
    nh, dh, page = cache_kt.shape[2:]
    groups = n_pages // pages

    def page_spec(p):
        return pl.BlockSpec(
            (None, None, nh, dh, page),
            lambda b, j, pt: (layer, pt[b, (groups - 1 - j) * pages + p], 0, 0, 0))

    grid_spec = pltpu.PrefetchScalarGridSpec(
        num_scalar_prefetch=1,
        grid=(db, groups),
        in_specs=[
            pl.BlockSpec((None, nh, dh, page), lambda b, j, pt: (b, 0, 0, 0)),
            pl.BlockSpec((nh, page), lambda b, j, pt: (0, 0)),
            pl.BlockSpec((page, page), lambda b, j, pt: (0, 0)),
        ] + [page_spec(p) for p in range(pages)] * 2,
        out_specs=pl.BlockSpec((None, nh, dh), lambda b, j, pt: (b, 0, 0)),
        scratch_shapes=[pltpu.VMEM((nh, dh, page), F32), pltpu.VMEM((nh, 1), F32)],
    )
    return pl.pallas_call(
        functools.partial(_attn_decode_kernel, pages=pages),
        grid_spec=grid_spec,
        out_shape=jax.ShapeDtypeStruct((db, nh, dh), F32),
        compiler_params=_params("parallel", "arbitrary"),
        name="attn_decode",
    )(page_table, q_rep, bias_rep, tri, *([cache_kt] * pages), *([cache_vt] * pages))


def _conv_tail(y, lng, lnb, pww, pwb):
    mu = jnp.mean(y, axis=-1, keepdims=True)
    yc = y - mu
    var = jnp.mean(yc * yc, axis=-1, keepdims=True)
    yn = yc * lax.rsqrt(var + EPS) * lng + lnb
    yn = (yn * jax.nn.sigmoid(yn)).astype(BF16)
    return jnp.dot(yn, pww, preferred_element_type=F32) + pwb


def _pool_window_of_lane(shape):
    lane = lax.broadcasted_iota(jnp.int32, shape, len(shape) - 1)
    gw = shape[-1] // len(POOL_WINDOWS)
    win = jnp.full(shape, POOL_WINDOWS[-1], jnp.int32)
    for g in reversed(range(len(POOL_WINDOWS) - 1)):
        win = jnp.where(lane < (g + 1) * gw, POOL_WINDOWS[g], win)
    return win


HALO = 32


def _mix_prompt_kernel(u_ref, up_ref, h_ref, hp_ref, poolw_ref, pscale_ref, dww_ref, dwb_ref,
                       lng_ref, lnb_ref, pww_ref, pwb_ref, pool_o, conv_o, uext, hext, ush, hsh):
    i = pl.program_id(1)
    tt, c = u_ref.shape

    @pl.when(i == 0)
    def _():
        uext[0:HALO, :] = jnp.zeros((HALO, c), F32)
        hext[0:HALO, :] = jnp.zeros((HALO, c), F32)

    @pl.when(i > 0)
    def _():
        uext[0:HALO, :] = up_ref[...]
        hext[0:HALO, :] = hp_ref[...]

    u = u_ref[...]
    uext[HALO:, :] = u
    hext[HALO:, :] = h_ref[...]

    for off in range(1, SUBLANES):
        ush[off - 1] = uext[pl.ds(off, HALO + tt - SUBLANES), :]
        hsh[off - 1] = hext[pl.ds(off, HALO + tt - SUBLANES), :]

    def window(ext, shifted, r0):
        a, off = divmod(r0, SUBLANES)
        if off == 0:
            return ext[pl.ds(r0, tt), :]
        return shifted[off - 1, pl.ds(a * SUBLANES, tt), :]

    sums = []
    run = u
    k = 1
    for wnd in POOL_WINDOWS:
        while k < wnd:
            run = run + window(uext, ush, HALO - k)
            k += 1
        sums.append(run)
    win = _pool_window_of_lane((tt, c))
    ssel = sums[-1]
    for g in reversed(range(len(POOL_WINDOWS) - 1)):
        ssel = jnp.where(win == POOL_WINDOWS[g], sums[g], ssel)
    pos = i * tt + lax.broadcasted_iota(jnp.int32, (tt, c), 0)
    cnt = jnp.minimum(win, pos + 1).astype(F32)
    d = (ssel / cnt - u).astype(BF16)
    pool_o[...] = (jnp.dot(d, poolw_ref[...], preferred_element_type=F32) * pscale_ref[...]).astype(BF16)

    y = jnp.zeros((tt, c), F32) + dwb_ref[...]
    for tap in range(CONV_TAPS):
        y = y + window(hext, hsh, HALO - CONV_HIST + tap) * dww_ref[tap:tap + 1, :]
    conv_o[...] = _conv_tail(y, lng_ref[...], lnb_ref[...], pww_ref[...], pwb_ref[...]).astype(BF16)


def _mix_prompt(u, h, poolw_bd, pscale, dww, dwb, lng, lnb, pww, pwb, *, tt):
    b, t, c = u.shape
    r = tt // HALO
    cur = pl.BlockSpec((None, tt, c), lambda bi, i: (bi, i, 0))
    prev = pl.BlockSpec((None, HALO, c), lambda bi, i: (bi, jnp.maximum(i * r - 1, 0), 0))
    vec = _const_spec((1, c))
    return pl.pallas_call(
        _mix_prompt_kernel,
        grid=(b, t // tt),
        in_specs=[cur, prev, cur, prev, _const_spec((c, c)), vec, _const_spec(dww.shape), vec,
                  vec, vec, _const_spec((c, c)), vec],
        out_specs=[cur, cur],
        out_shape=[jax.ShapeDtypeStruct((b, t, c), BF16)] * 2,
        scratch_shapes=[pltpu.VMEM((HALO + tt, c), F32)] * 2
        + [pltpu.VMEM((SUBLANES - 1, HALO + tt - SUBLANES, c), F32)] * 2,
        compiler_params=_params("parallel", "arbitrary"),
        name="mix_prompt",
    )(u, u, h, h, poolw_bd, pscale, dww, dwb, lng, lnb, pww, pwb)


def _mix_decode_kernel(u_ref, sp_ref, h_ref, sc_ref, poolw_ref, pscale_ref, dww_ref, dwb_ref,
                       lng_ref, lnb_ref, pww_ref, pwb_ref, pool_o, conv_o, *, start_pos):
    u = u_ref[...]
    sums = []
    run = u
    k = 1
    for wnd in POOL_WINDOWS:
        while k < wnd:
            run = run + sp_ref[POOL_HIST - k]
            k += 1
        sums.append(run)
    win = _pool_window_of_lane(u.shape)
    ssel = sums[-1]
    for g in reversed(range(len(POOL_WINDOWS) - 1)):
        ssel = jnp.where(win == POOL_WINDOWS[g], sums[g], ssel)
    cnt = jnp.minimum(win, start_pos + 1).astype(F32)
    d = (ssel / cnt - u).astype(BF16)
    pool_o[...] = (jnp.dot(d, poolw_ref[...], preferred_element_type=F32) * pscale_ref[...]).astype(BF16)

    y = h_ref[...] * dww_ref[CONV_HIST:CONV_TAPS, :] + dwb_ref[...]
    for tap in range(CONV_HIST):
        y = y + sc_ref[tap] * dww_ref[tap:tap + 1, :]
    conv_o[...] = _conv_tail(y, lng_ref[...], lnb_ref[...], pww_ref[...], pwb_ref[...]).astype(BF16)


def _mix_decode(u, state_pool_tm, h, state_conv_tm, poolw_bd, pscale, dww, dwb, lng, lnb, pww, pwb, *, start_pos):
    db, c = u.shape
    args = (u, state_pool_tm, h, state_conv_tm, poolw_bd, pscale, dww, dwb, lng, lnb, pww, pwb)
    return pl.pallas_call(
        functools.partial(_mix_decode_kernel, start_pos=start_pos),
        grid=(1,),
        in_specs=[_const_spec(a.shape) for a in args],
        out_specs=[_const_spec((db, c))] * 2,
        out_shape=[jax.ShapeDtypeStruct((db, c), BF16)] * 2,
        compiler_params=_params("arbitrary"),
        name="mix_decode",
    )(*args)


def _merge_ffn_kernel(x_ref, att_ref, pool_ref, conv_ref, wo_ref, g_ref, wup_ref, wdn_ref, o_ref, *, ff_chunk):
    cp = pool_ref.shape[1]
    if len(att_ref.shape) == 3:
        nh, dh, rows = att_ref.shape
        att = att_ref[...].reshape(nh * dh, rows).T.astype(BF16)
    else:
        att = att_ref[...]
    ca = att.shape[1]
    m = jnp.dot(att, wo_ref[0:ca, :], preferred_element_type=F32)
    m = m + jnp.dot(pool_ref[...], wo_ref[ca:ca + cp, :], preferred_element_type=F32)
    m = m + jnp.dot(conv_ref[...], wo_ref[ca + cp:, :], preferred_element_type=F32)
    x1 = x_ref[...] + m
    ms = jnp.mean(x1 * x1, axis=-1, keepdims=True)
    xn = (x1 * lax.rsqrt(ms + EPS) * g_ref[...]).astype(BF16)
    out = x1
    d_ff = wup_ref.shape[1]
    for c0 in range(0, d_ff, ff_chunk):
        hmid = jnp.dot(xn, wup_ref[:, c0:c0 + ff_chunk], preferred_element_type=F32)
        hmid = jnp.square(jnp.maximum(hmid, 0.0)).astype(BF16)
        out = out + jnp.dot(hmid, wdn_ref[c0:c0 + ff_chunk, :], preferred_element_type=F32)
    o_ref[...] = out


def _merge_ffn(x, att, pool, conv, wo_bf, g, wup_bf, wdn_bf, *, tm, ff_chunk):
    m, d = x.shape
    row = lambda c: pl.BlockSpec((tm, c), lambda i: (i, 0))
    single = lambda shape: pl.BlockSpec(shape, lambda i: (0, 0), pipeline_mode=pl.Buffered(1))
    if att.ndim == 4:
        nt = att.shape[3] // tm
        att_spec = pl.BlockSpec((None,) + att.shape[1:3] + (tm,), lambda i: (i // nt, 0, 0, i % nt))
    else:
        att_spec = row(att.shape[1])
    return pl.pallas_call(
        functools.partial(_merge_ffn_kernel, ff_chunk=ff_chunk),
        grid=(m // tm,),
        in_specs=[row(d), att_spec, row(pool.shape[1]), row(conv.shape[1]),
                  single(wo_bf.shape), single((1, d)), single(wup_bf.shape), single(wdn_bf.shape)],
        out_specs=row(d),
        out_shape=jax.ShapeDtypeStruct((m, d), F32),
        compiler_params=_params("parallel"),
        name="merge_ffn",
    )(x, att, pool, conv, wo_bf, g, wup_bf, wdn_bf)


def _block_diag(blocks):
    g, a, b = blocks.shape
    eye = jnp.eye(g, dtype=blocks.dtype)
    return (eye[:, None, :, None] * blocks[:, :, None, :]).reshape(g * a, g * b)


def _reverse_tri(n):
    return (jnp.arange(n)[None, :] >= jnp.arange(n)[:, None]).astype(BF16)


ATTN_TQ = 256
ATTN_HEADS_PER_STEP = 8
PROMPT_TM = 256
FFN_TM = 512
MIX_TT = 512
FF_CHUNK = 1024
DECODE_PAGES = 32
DECODE_PAGE_GROUP = 4


def kernel(x_prompt, x_sample, cache_k, cache_v, state_pool, state_conv, page_table, norm_mix_g, w_in, q_norm_g, k_norm_g, sb_bias, pool_w, pool_scale, conv_dw_w, conv_dw_b, conv_ln_g, conv_ln_b, conv_pw_w, conv_pw_b, w_out, norm_ffn_g, w_ffn_up, w_ffn_down):
    b, t, d = x_prompt.shape
    db, ts, _ = x_sample.shape
    assert ts == 1
    assert PROMPT_TM == ATTN_TQ
    depth = w_in.shape[0]
    n_heads = sb_bias.shape[1]
    attn_ch = n_heads * HEAD_DIM
    pool_ch = pool_scale.shape[1]
    conv_ch = conv_dw_b.shape[1]
    past_len = page_table.shape[1] * PAGE_SIZE

    xp = x_prompt.reshape(b * t, d)
    xs = x_sample.reshape(db, d)

    head_mean = _block_diag(jnp.full((n_heads, HEAD_DIM, HEAD_DIM), 1.0 / HEAD_DIM, BF16))
    tri_prompt = _reverse_tri(ATTN_TQ)
    tri_page = _reverse_tri(PAGE_SIZE).T
    cache_kt = cache_k.transpose(0, 1, 3, 4, 2)
    cache_vt = cache_v.transpose(0, 1, 3, 4, 2)
    ch = jnp.arange(attn_ch)
    place = (jnp.arange(n_heads * LANES)[None, :] == ((ch // HEAD_DIM) * LANES + ch % HEAD_DIM)[:, None]).astype(BF16)
    qpad = jnp.zeros((LANES - HEAD_DIM, PROMPT_TM), BF16).at[0:2, :].set(1.0)
    row2 = lambda v: v.reshape(1, -1)

    outs = {n: [] for n in ("ks", "vs", "pp", "ps", "cp", "cs")}
    kv_all = None
    for l in range(depth):
        poolw_bd = _block_diag(pool_w[l]).astype(BF16)
        mix_w = (poolw_bd, row2(pool_scale[l]), conv_dw_w[l], row2(conv_dw_b[l]),
                 row2(conv_ln_g[l]), row2(conv_ln_b[l]), conv_pw_w[l].astype(BF16), row2(conv_pw_b[l]))
        in_w = (row2(norm_mix_g[l]), w_in[l].astype(BF16), row2(jnp.tile(q_norm_g[l], n_heads)),
                row2(jnp.tile(k_norm_g[l], n_heads)), head_mean)
        dims = dict(attn_ch=attn_ch, pool_ch=pool_ch, conv_ch=conv_ch)
        ffn_w = (w_out[l].astype(BF16), row2(norm_ffn_g[l]), w_ffn_up[l].astype(BF16), w_ffn_down[l].astype(BF16))
        bias2 = sb_bias[l] * LOG2E
        bias_hi = bias2.astype(BF16)
        bias_lo = (bias2 - bias_hi.astype(F32)).astype(BF16)
        kpad = jnp.zeros((n_heads, LANES), F32).at[:, HEAD_DIM].set(bias_hi.astype(F32))
        kpad = kpad.at[:, HEAD_DIM + 1].set(bias_lo.astype(F32)).reshape(1, n_heads * LANES)

        qt, k_aug, vt, kt_all, vt_all, up, hp = _inproj_prompt(
            x_prompt if l == 0 else xp.reshape(b, t, d), *in_w, place, kpad, qpad, kv_all,
            layer=l, depth=depth, tm=PROMPT_TM, **dims)
        kv_all = (kt_all, vt_all)
        qs, ksf, vsf, us, hs = _inproj_decode(xs, *in_w, **dims)

        att_t = _attn_prompt(qt, k_aug, vt, tri_prompt, tq=ATTN_TQ, heads=ATTN_HEADS_PER_STEP)

        q_rep = jnp.broadcast_to(qs.reshape(db, n_heads, HEAD_DIM, 1), (db, n_heads, HEAD_DIM, PAGE_SIZE))
        bias_rep = jnp.broadcast_to(bias2[:, None], (n_heads, PAGE_SIZE))
        att_s = _attn_decode(page_table, q_rep, bias_rep, tri_page, cache_kt, cache_vt,
                             layer=l, pages=DECODE_PAGES).reshape(db, attn_ch).astype(BF16)

        pool_p, conv_p = _mix_prompt(up, hp, *mix_w, tt=MIX_TT)
        pool_s, conv_s = _mix_decode(us, state_pool[l].transpose(1, 0, 2), hs, state_conv[l].transpose(1, 0, 2),
                                     *mix_w, start_pos=past_len)

        xp = _merge_ffn(xp, att_t, pool_p.reshape(b * t, pool_ch), conv_p.reshape(b * t, conv_ch), *ffn_w,
                        tm=FFN_TM, ff_chunk=FF_CHUNK)
        xs = _merge_ffn(xs, att_s, pool_s, conv_s, *ffn_w, tm=db, ff_chunk=FF_CHUNK)

        outs["ks"].append(ksf.reshape(db, ts, n_heads, HEAD_DIM))
        outs["vs"].append(vsf.reshape(db, ts, n_heads, HEAD_DIM))
        outs["pp"].append(up[:, t - POOL_HIST:])
        outs["ps"].append(jnp.concatenate([state_pool[l][:, 1:], us[:, None, :]], axis=1))
        outs["cp"].append(hp[:, t - CONV_HIST:])
        outs["cs"].append(jnp.concatenate([state_conv[l][:, 1:], hs[:, None, :]], axis=1))

    return (xp.reshape(b, t, d), xs.reshape(db, ts, d),
            kv_all[0].transpose(0, 1, 4, 2, 3), kv_all[1].transpose(0, 1, 4, 2, 3),
            jnp.stack(outs["ks"]), jnp.stack(outs["vs"]),
            jnp.stack(outs["pp"]), jnp.stack(outs["ps"]), jnp.stack(outs["cp"]), jnp.stack(outs["cs"]))
```
